```python
import math
import jax, jax.numpy as jnp
from jax import lax
import numpy as np

D_MODEL = 1024
BATCH = 32
SEQ = 2048
DEPTH = 1

N_META = 16
BLOCK_Q = 128
HEAD_DIM = 64
H_SB = 8
H_FOX = 8
W_SB = H_SB * HEAD_DIM
W_FOX = H_FOX * HEAD_DIM
D_FF = ((8 * D_MODEL // 3 + 127) // 128) * 128
CONV_W = 3
RMS_EPS = 1e-6
IN_SIZES = (W_SB, W_SB, W_SB, W_FOX, W_FOX, W_FOX, H_FOX, D_MODEL, D_MODEL)
IN_COLS = 3 * W_SB + 3 * W_FOX + H_FOX + 2 * D_MODEL

kernel_name = "hybrid_stickbreak_fox_convffn"


def rmsnorm(x, g):
    xf = x.astype(jnp.float32)
    r = lax.rsqrt(jnp.mean(xf * xf, axis=-1, keepdims=True) + RMS_EPS)
    return (xf * r).astype(x.dtype) * g


def split_heads(t, n_heads):
    b, l, _ = t.shape
    return t.reshape(b, l, n_heads, HEAD_DIM).transpose(0, 2, 1, 3)


def merge_heads(t):
    b, h, l, d = t.shape
    return t.transpose(0, 2, 1, 3).reshape(b, l, h * d)


def query_blocks(total_len):
    bounds = [(0, min(N_META, total_len))]
    for s in range(N_META, total_len, BLOCK_Q):
        bounds.append((s, min(s + BLOCK_Q, total_len)))
    return bounds


def stick_breaking_block(q, k, v, start):
    tq, tk = q.shape[2], k.shape[2]
    z = jnp.einsum('bhqd,bhkd->bhqk', q, k).astype(jnp.float32) / math.sqrt(HEAD_DIM)
    t_idx = start + jnp.arange(tq)[:, None]
    s_idx = jnp.arange(tk)[None, :]
    strict = s_idx < t_idx
    log_beta = jax.nn.log_sigmoid(z)
    log_keep = jnp.where(strict, jax.nn.log_sigmoid(-z), 0.0)
    after = lax.cumsum(log_keep, axis=3, reverse=True) - log_keep
    w = jnp.where(strict, jnp.exp(log_beta + after), 0.0)
    return jnp.einsum('bhqk,bhkd->bhqd', w.astype(v.dtype), v)


def forgetting_block(q, k, v, cum_log_f, start):
    tq, tk = q.shape[2], k.shape[2]
    z = jnp.einsum('bhqd,bhkd->bhqk', q, k).astype(jnp.float32) / math.sqrt(HEAD_DIM)
    z = z + cum_log_f[:, :, start:start + tq, None] - cum_log_f[:, :, None, :tk]
    t_idx = start + jnp.arange(tq)[:, None]
    s_idx = jnp.arange(tk)[None, :]
    z = jnp.where(s_idx <= t_idx, z, -jnp.inf)
    p = jax.nn.softmax(z, axis=-1)
    return jnp.einsum('bhqk,bhkd->bhqd', p.astype(v.dtype), v)


def hybrid_mixer(h, w_in, b_forget, w_branch_sb, w_branch_fox, w_out):
    total_len = h.shape[1]
    proj = h @ w_in
    split_at = np.cumsum(IN_SIZES)[:-1].tolist()
    q_sb, k_sb, v_sb, q_fx, k_fx, v_fx, f_logit, g_sb, g_fx = jnp.split(proj, split_at, axis=-1)
    q_sb, k_sb, v_sb = split_heads(q_sb, H_SB), split_heads(k_sb, H_SB), split_heads(v_sb, H_SB)
    q_fx, k_fx, v_fx = split_heads(q_fx, H_FOX), split_heads(k_fx, H_FOX), split_heads(v_fx, H_FOX)
    log_f = jax.nn.log_sigmoid(f_logit.astype(jnp.float32) + b_forget.astype(jnp.float32))
    cum_log_f = jnp.cumsum(log_f, axis=1).transpose(0, 2, 1)

    out_sb, out_fx = [], []
    for start, end in query_blocks(total_len):
        out_sb.append(stick_breaking_block(q_sb[:, :, start:end], k_sb[:, :, :end], v_sb[:, :, :end], start))
        out_fx.append(forgetting_block(q_fx[:, :, start:end], k_fx[:, :, :end], v_fx[:, :, :end], cum_log_f, start))
    o_sb = merge_heads(jnp.concatenate(out_sb, axis=2))
    o_fx = merge_heads(jnp.concatenate(out_fx, axis=2))

    merged = jax.nn.sigmoid(g_sb) * (o_sb @ w_branch_sb) + jax.nn.sigmoid(g_fx) * (o_fx @ w_branch_fox)
    return merged @ w_out


def conv_ffn(h, w_up, conv_w, w_down):
    total_len = h.shape[1]
    u = h @ w_up
    u_pad = jnp.pad(u, ((0, 0), (CONV_W - 1, 0), (0, 0)))
    uc = sum(conv_w[j] * u_pad[:, j:j + total_len] for j in range(CONV_W))
    a, b = jnp.split(uc, 2, axis=-1)
    return (jax.nn.silu(a) * b) @ w_down


def setup_inputs(seed: int = 0) -> dict:
    key = jax.random.key(seed)
    ks = jax.random.split(key, 14)
    f32 = jnp.float32
    x = jax.random.normal(ks[0], (BATCH, SEQ, D_MODEL), f32)
    meta_tokens = jax.random.normal(ks[1], (N_META, D_MODEL), f32)
    norm_mix_g = 1.0 + 0.02 * jax.random.normal(ks[2], (DEPTH, D_MODEL), f32)
    w_in = jax.random.normal(ks[3], (DEPTH, D_MODEL, IN_COLS), f32) * D_MODEL ** -0.5
    b_forget = jnp.linspace(1.0, 6.0, H_FOX, dtype=f32)[None, :] + 0.1 * jax.random.normal(ks[4], (DEPTH, H_FOX), f32)
    w_branch_sb = jax.random.normal(ks[5], (DEPTH, W_SB, D_MODEL), f32) * W_SB ** -0.5
    w_branch_fox = jax.random.normal(ks[6], (DEPTH, W_FOX, D_MODEL), f32) * W_FOX ** -0.5
    w_out = jax.random.normal(ks[7], (DEPTH, D_MODEL, D_MODEL), f32) * D_MODEL ** -0.5
    norm_ffn_g = 1.0 + 0.02 * jax.random.normal(ks[8], (DEPTH, D_MODEL), f32)
    w_up = jax.random.normal(ks[9], (DEPTH, D_MODEL, 2 * D_FF), f32) * D_MODEL ** -0.5
    conv_w = jax.random.normal(ks[10], (DEPTH, CONV_W, 2 * D_FF), f32) * CONV_W ** -0.5
    w_down = jax.random.normal(ks[11], (DEPTH, D_FF, D_MODEL), f32) * D_FF ** -0.5
    norm_final_g = 1.0 + 0.02 * jax.random.normal(ks[12], (D_MODEL,), f32)
    return {"x": x, "meta_tokens": meta_tokens, "norm_mix_g": norm_mix_g, "w_in": w_in,
            "b_forget": b_forget, "w_branch_sb": w_branch_sb, "w_branch_fox": w_branch_fox,
            "w_out": w_out, "norm_ffn_g": norm_ffn_g, "w_up": w_up, "conv_w": conv_w,
            "w_down": w_down, "norm_final_g": norm_final_g}


def reference(x, meta_tokens, norm_mix_g, w_in, b_forget, w_branch_sb, w_branch_fox,
              w_out, norm_ffn_g, w_up, conv_w, w_down, norm_final_g):
    b = x.shape[0]
    meta = jnp.broadcast_to(meta_tokens[None].astype(x.dtype), (b, N_META, D_MODEL))
    h = jnp.concatenate([meta, x], axis=1)
    for layer in range(DEPTH):
        h = h + hybrid_mixer(rmsnorm(h, norm_mix_g[layer]), w_in[layer], b_forget[layer],
                             w_branch_sb[layer], w_branch_fox[layer], w_out[layer])
        h = h + conv_ffn(rmsnorm(h, norm_ffn_g[layer]), w_up[layer], conv_w[layer], w_down[layer])
    return rmsnorm(h, norm_final_g)[:, N_META:]
```

```python
import functools

import jax
import jax.numpy as jnp
from jax import lax
from jax.experimental import pallas as pl
from jax.experimental.pallas import tpu as pltpu

D_MODEL = 1024
SEQ = 2048
N_META = 16
HEAD_DIM = 64
N_HEADS = 8
W_BRANCH = N_HEADS * HEAD_DIM
D_FF = 2816
CONV_W = 3
RMS_EPS = 1e-6

LANES = 128
SUBLANES = 8
BLK = 128
LP = 2176
N_BLK = LP // BLK
HEAD_PAIRS = N_HEADS // 2
QKV_COLS = 6 * W_BRANCH
TM_IN = 512
TM_OUT = LP // 4
TM_FFN = LP // 4
FF_CHUNK = 256
VMEM_LIMIT = 56 * 1024 * 1024

F32 = jnp.float32
BF16 = jnp.bfloat16
NEG_BIG = -1e30


def _dot(a, b):
    return jnp.dot(a, b, preferred_element_type=F32)


def _dot_nt(a, b):
    return lax.dot_general(a, b, (((1,), (1,)), ((), ())), preferred_element_type=F32)


def _rms_scale(x):
    return lax.rsqrt(jnp.mean(x * x, axis=-1, keepdims=True) + RMS_EPS)


def _sigmoid(x):
    return 1.0 / (1.0 + jnp.exp(-x))


def _split3(x):
    hi = x.astype(BF16)
    r1 = x - hi.astype(F32)
    mid = r1.astype(BF16)
    lo = (r1 - mid.astype(F32)).astype(BF16)
    return hi, mid, lo


def _inproj_kernel(h_ref, g_ref, wqkv_ref, wg_ref, wf_ref, bf_ref, qkv_ref, gate_ref, lf_ref):
    x = h_ref[...]
    hn = ((x * _rms_scale(x)) * g_ref[...]).astype(BF16)
    for c in range(6):
        cols = slice(c * W_BRANCH, (c + 1) * W_BRANCH)
        y = _dot(hn, wqkv_ref[:, cols])
        if c in (0, 3):
            y = y * (HEAD_DIM ** -0.5)
        qkv_ref[:, cols] = y.astype(BF16)
    gate_ref[...] = _dot(hn, wg_ref[...])
    f = _dot(hn, wf_ref[...]) + bf_ref[...]
    lf_ref[...] = jnp.minimum(f, 0.0) - jnp.log(1.0 + jnp.exp(-jnp.abs(f)))


def _cumsum_kernel(lf_ref, tri_ref, c_ref):
    tri = tri_ref[...]
    carry = jnp.zeros((1, LANES), F32)
    for j in range(N_BLK):
        rows = slice(j * BLK, (j + 1) * BLK)
        hi, mid, lo = _split3(lf_ref[rows, :])
        cs = (_dot(tri, hi) + _dot(tri, mid)) + _dot(tri, lo) + carry
        c_ref[rows, :] = cs
        carry = cs[BLK - 1:BLK, :]


def _head_queries(q_ref):
    q2 = q_ref[...].astype(F32)
    lane = lax.broadcasted_iota(jnp.int32, q2.shape, 1)
    first = lane < HEAD_DIM
    return (jnp.where(first, q2, 0.0).astype(BF16), jnp.where(first, 0.0, q2).astype(BF16))


def _merge_heads(acc0, acc1):
    lane = lax.broadcasted_iota(jnp.int32, acc0.shape, 1)
    return jnp.where(lane < HEAD_DIM, acc0, acc1)


def _sb_kernel(q_ref, k_ref, v_ref, u_ref, o_ref):
    qi = pl.program_id(2)
    qs = _head_queries(q_ref)
    u = u_ref[...]
    row = lax.broadcasted_iota(jnp.int32, (BLK, BLK), 0)
    col = lax.broadcasted_iota(jnp.int32, (BLK, BLK), 1)
    strict = col < row

    def tile(j, state, diagonal):
        start = pl.multiple_of(j * BLK, BLK)
        kb = k_ref[pl.ds(start, BLK), :]
        vb = v_ref[pl.ds(start, BLK), :]
        new = []
        for h in range(2):
            acc, carry = state[2 * h], state[2 * h + 1]
            z = _dot_nt(qs[h], kb)
            nlk = jnp.maximum(z, 0.0) + jnp.log(1.0 + jnp.exp(-jnp.abs(z)))
            if diagonal:
                nlk = jnp.where(strict, nlk, 0.0)
            hi = nlk.astype(BF16)
            lo = (nlk - hi.astype(F32)).astype(BF16)
            cl = _dot(hi, u) + _dot(lo, u)
            w = jnp.exp(z - cl - carry)
            if diagonal:
                w = jnp.where(strict, w, 0.0)
            new.append(acc + _dot(w.astype(BF16), vb))
            new.append(carry + cl[:, 0:1])
        return tuple(new)

    zero_acc = jnp.zeros((BLK, LANES), F32)
    zero_col = jnp.zeros((BLK, 1), F32)
    state = tile(qi, (zero_acc, zero_col, zero_acc, zero_col), True)
    state = lax.fori_loop(0, qi, lambda i, s: tile(qi - 1 - i, s, False), state)
    o_ref[...] = _merge_heads(state[0], state[2]).astype(BF16)


def _fox_kernel(q_ref, k_ref, v_ref, ccol_ref, crow_ref, o_ref):
    qi = pl.program_id(2)
    qs = _head_queries(q_ref)
    row = lax.broadcasted_iota(jnp.int32, (BLK, BLK), 0)
    col = lax.broadcasted_iota(jnp.int32, (BLK, BLK), 1)
    causal = col <= row
    ct = [ccol_ref[0, 0, :, h:h + 1] for h in range(2)]

    def tile(j, state, diagonal):
        start = pl.multiple_of(j * BLK, BLK)
        kb = k_ref[pl.ds(start, BLK), :]
        vb = v_ref[pl.ds(start, BLK), :]
        new = []
        for h in range(2):
            acc, m, l = state[3 * h], state[3 * h + 1], state[3 * h + 2]
            cs = crow_ref[0, 0, h:h + 1, pl.ds(start, BLK)]
            s = _dot_nt(qs[h], kb) + (ct[h] - cs)
            if diagonal:
                s = jnp.where(causal, s, NEG_BIG)
            m_new = jnp.maximum(m, jnp.max(s, axis=1, keepdims=True))
            alpha = jnp.exp(m - m_new)
            p = jnp.exp(s - m_new)
            new.append(alpha * acc + _dot(p.astype(BF16), vb))
            new.append(m_new)
            new.append(alpha * l + jnp.sum(p, axis=1, keepdims=True))
        return tuple(new)

    zero_acc = jnp.zeros((BLK, LANES), F32)
    init = (zero_acc, jnp.full((BLK, 1), NEG_BIG, F32), jnp.zeros((BLK, 1), F32))
    state = tile(qi, init + init, True)
    state = lax.fori_loop(0, qi, lambda i, s: tile(qi - 1 - i, s, False), state)
    o_ref[...] = _merge_heads(state[0] / state[2], state[3] / state[5]).astype(BF16)


def _outproj_kernel(h_ref, osb_ref, ofx_ref, gate_ref, wbs_ref, wbf_ref, wo_ref, g2_ref,
                    h1_ref, hn2_ref):
    t_sb = _dot(osb_ref[...], wbs_ref[...])
    t_fx = _dot(ofx_ref[...], wbf_ref[...])
    merged = (_sigmoid(gate_ref[:, :D_MODEL]) * t_sb + _sigmoid(gate_ref[:, D_MODEL:]) * t_fx)
    h1 = h_ref[...] + _dot(merged.astype(BF16), wo_ref[...])
    h1_ref[...] = h1
    hn2 = (h1 * _rms_scale(h1)) * g2_ref[...]
    row = (pl.program_id(0) % (LP // TM_OUT)) * TM_OUT + lax.broadcasted_iota(
        jnp.int32, (TM_OUT, 1), 0)
    hn2_ref[...] = jnp.where(row < N_META + SEQ, hn2, 0.0).astype(BF16)


def _ffn_kernel(hn2_ref, h1_ref, wup_ref, cw_ref, wdn_ref, gf_ref, out_ref,
                ua_ref, ub_ref, hist_ref, acc_ref):
    @pl.when(pl.program_id(0) == 0)
    def _():
        hist_ref[...] = jnp.zeros_like(hist_ref)

    hn2 = hn2_ref[...]
    tm = hn2.shape[0]
    for c in range(D_FF // FF_CHUNK):
        parts = []
        for half, u_ref in ((0, ua_ref), (1, ub_ref)):
            cols = slice(half * D_FF + c * FF_CHUNK, half * D_FF + (c + 1) * FF_CHUNK)
            u = _dot(hn2, wup_ref[:, cols])
            u_ref[0:SUBLANES, :] = hist_ref[:, cols]
            u_ref[SUBLANES:SUBLANES + tm, :] = u
            hist_ref[:, cols] = u[tm - SUBLANES:, :]
            cw = cw_ref[:, cols]
            parts.append(cw[0:1, :] * u_ref[SUBLANES - 2:SUBLANES - 2 + tm, :]
                         + cw[1:2, :] * u_ref[SUBLANES - 1:SUBLANES - 1 + tm, :]
                         + cw[2:3, :] * u)
        a, b = parts
        gated = (a * _sigmoid(a) * b).astype(BF16)
        contrib = _dot(gated, wdn_ref[c * FF_CHUNK:(c + 1) * FF_CHUNK, :])
        if c == 0:
            acc_ref[...] = contrib
        else:
            acc_ref[...] += contrib
    h2 = h1_ref[...] + acc_ref[...]
    out_ref[...] = (h2 * _rms_scale(h2)) * gf_ref[...]


def _const_spec(shape):
    return pl.BlockSpec(shape, lambda *_: (0,) * len(shape))


def _params(semantics):
    return pltpu.CompilerParams(dimension_semantics=semantics, vmem_limit_bytes=VMEM_LIMIT)


def kernel(x, meta_tokens, norm_mix_g, w_in, b_forget, w_branch_sb, w_branch_fox, w_out,
           norm_ffn_g, w_up, conv_w, w_down, norm_final_g):
    batch = x.shape[0]
    assert x.shape == (batch, SEQ, D_MODEL) and w_in.shape[0] == 1
    rows = batch * LP
    assert rows % TM_IN == 0

    meta = jnp.broadcast_to(meta_tokens[None].astype(x.dtype), (batch, N_META, D_MODEL))
    pad = jnp.zeros((batch, LP - N_META - SEQ, D_MODEL), x.dtype)
    h = jnp.concatenate([meta, x, pad], axis=1).reshape(rows, D_MODEL)

    w_in0 = w_in[0]
    w_qkv = w_in0[:, :QKV_COLS].astype(BF16)
    w_f = jnp.pad(w_in0[:, QKV_COLS:QKV_COLS + N_HEADS], ((0, 0), (0, LANES - N_HEADS))).astype(BF16)
    w_g = w_in0[:, QKV_COLS + N_HEADS:].astype(BF16)
    b_f = jnp.pad(b_forget[0].astype(F32), (0, LANES - N_HEADS)).reshape(1, LANES)

    qkv, gate, lf = pl.pallas_call(
        _inproj_kernel,
        grid=(rows // TM_IN,),
        in_specs=[
            pl.BlockSpec((TM_IN, D_MODEL), lambda i: (i, 0)),
            _const_spec((1, D_MODEL)),
            _const_spec((D_MODEL, QKV_COLS)),
            _const_spec((D_MODEL, 2 * D_MODEL)),
            _const_spec((D_MODEL, LANES)),
            _const_spec((1, LANES)),
        ],
        out_specs=[
            pl.BlockSpec((TM_IN, QKV_COLS), lambda i: (i, 0)),
            pl.BlockSpec((TM_IN, 2 * D_MODEL), lambda i: (i, 0)),
            pl.BlockSpec((TM_IN, LANES), lambda i: (i, 0)),
        ],
        out_shape=[
            jax.ShapeDtypeStruct((rows, QKV_COLS), BF16),
            jax.ShapeDtypeStruct((rows, 2 * D_MODEL), F32),
            jax.ShapeDtypeStruct((rows, LANES), F32),
        ],
        compiler_params=_params(("parallel",)),
        name="inproj",
    )(h, norm_mix_g[0].reshape(1, D_MODEL), w_qkv, w_g, w_f, b_f)

    idx = jnp.arange(BLK)
    tri = (idx[None, :] <= idx[:, None]).astype(BF16)
    cum = pl.pallas_call(
        _cumsum_kernel,
        grid=(batch,),
        in_specs=[pl.BlockSpec((LP, LANES), lambda b: (b, 0)), _const_spec((BLK, BLK))],
        out_specs=pl.BlockSpec((LP, LANES), lambda b: (b, 0)),
        out_shape=jax.ShapeDtypeStruct((rows, LANES), F32),
        compiler_params=_params(("parallel",)),
        name="forget_cumsum",
    )(lf, tri)
    cum = cum[:, :N_HEADS].reshape(batch, LP, HEAD_PAIRS, 2)
    c_col = cum.transpose(0, 2, 1, 3)
    c_row = cum.transpose(0, 2, 3, 1)

    def q_spec(base):
        return pl.BlockSpec((BLK, LANES), lambda b, p, q: (b * N_BLK + q, base + p))

    def kv_spec(base):
        return pl.BlockSpec((LP, LANES), lambda b, p, q: (b, base + p))

    o_spec = pl.BlockSpec((BLK, LANES), lambda b, p, q: (b * N_BLK + q, p))
    o_shape = jax.ShapeDtypeStruct((rows, W_BRANCH), BF16)
    attn_grid = (batch, HEAD_PAIRS, N_BLK)
    attn_params = _params(("parallel", "parallel", "arbitrary"))

    u_mat = (idx[:, None] >= idx[None, :]).astype(BF16)
    o_sb = pl.pallas_call(
        _sb_kernel,
        grid=attn_grid,
        in_specs=[q_spec(0), kv_spec(HEAD_PAIRS), kv_spec(2 * HEAD_PAIRS),
                  pl.BlockSpec((BLK, BLK), lambda b, p, q: (0, 0))],
        out_specs=o_spec,
        out_shape=o_shape,
        compiler_params=attn_params,
        name="stickbreak_attn",
    )(qkv, qkv, qkv, u_mat)

    o_fx = pl.pallas_call(
        _fox_kernel,
        grid=attn_grid,
        in_specs=[q_spec(3 * HEAD_PAIRS), kv_spec(4 * HEAD_PAIRS), kv_spec(5 * HEAD_PAIRS),
                  pl.BlockSpec((1, 1, BLK, 2), lambda b, p, q: (b, p, q, 0)),
                  pl.BlockSpec((1, 1, 2, LP), lambda b, p, q: (b, p, 0, 0))],
        out_specs=o_spec,
        out_shape=o_shape,
        compiler_params=attn_params,
        name="forgetting_attn",
    )(qkv, qkv, qkv, c_col, c_row)

    h1, hn2 = pl.pallas_call(
        _outproj_kernel,
        grid=(rows // TM_OUT,),
        in_specs=[
            pl.BlockSpec((TM_OUT, D_MODEL), lambda i: (i, 0)),
            pl.BlockSpec((TM_OUT, W_BRANCH), lambda i: (i, 0)),
            pl.BlockSpec((TM_OUT, W_BRANCH), lambda i: (i, 0)),
            pl.BlockSpec((TM_OUT, 2 * D_MODEL), lambda i: (i, 0)),
            _const_spec((W_BRANCH, D_MODEL)),
            _const_spec((W_BRANCH, D_MODEL)),
            _const_spec((D_MODEL, D_MODEL)),
            _const_spec((1, D_MODEL)),
        ],
        out_specs=[
            pl.BlockSpec((TM_OUT, D_MODEL), lambda i: (i, 0)),
            pl.BlockSpec((TM_OUT, D_MODEL), lambda i: (i, 0)),
        ],
        out_shape=[
            jax.ShapeDtypeStruct((rows, D_MODEL), F32),
            jax.ShapeDtypeStruct((rows, D_MODEL), BF16),
        ],
        compiler_params=_params(("parallel",)),
        name="outproj",
    )(h, o_sb, o_fx, gate, w_branch_sb[0].astype(BF16), w_branch_fox[0].astype(BF16),
      w_out[0].astype(BF16), norm_ffn_g[0].reshape(1, D_MODEL))

    out = pl.pallas_call(
        _ffn_kernel,
        grid=(rows // TM_FFN,),
        in_specs=[
            pl.BlockSpec((TM_FFN, D_MODEL), lambda i: (i, 0)),
            pl.BlockSpec((TM_FFN, D_MODEL), lambda i: (i, 0)),
            pl.BlockSpec((D_MODEL, 2 * D_FF), lambda i: (0, 0), pipeline_mode=pl.Buffered(1)),
            _const_spec((CONV_W, 2 * D_FF)),
            pl.BlockSpec((D_FF, D_MODEL), lambda i: (0, 0), pipeline_mode=pl.Buffered(1)),
            _const_spec((1, D_MODEL)),
        ],
        out_specs=pl.BlockSpec((TM_FFN, D_MODEL), lambda i: (i, 0)),
        out_shape=jax.ShapeDtypeStruct((rows, D_MODEL), F32),
        scratch_shapes=[
            pltpu.VMEM((SUBLANES + TM_FFN, FF_CHUNK), F32),
            pltpu.VMEM((SUBLANES + TM_FFN, FF_CHUNK), F32),
            pltpu.VMEM((SUBLANES, 2 * D_FF), F32),
            pltpu.VMEM((TM_FFN, D_MODEL), F32),
        ],
        compiler_params=_params(("arbitrary",)),
        name="conv_ffn",
    )(hn2, h1, w_up[0].astype(BF16), conv_w[0].astype(F32), w_down[0].astype(BF16),
      norm_final_g.reshape(1, D_MODEL))

    return out.reshape(batch, LP, D_MODEL)[:, N_META:N_META + SEQ]
```

```python
import jax
import jax.numpy as jnp
from jax import lax
from jax.experimental import pallas as pl
from jax.experimental.pallas import tpu as pltpu

D_MODEL = 1024
SEQ = 2048
N_META = 16
HEAD_DIM = 64
N_HEADS = 8
W_BRANCH = N_HEADS * HEAD_DIM
D_FF = 2816
CONV_W = 3
RMS_EPS = 1e-6

LANES = 128
SUBLANES = 8
META_ROWS = 128
HEAD_PAIRS = N_HEADS // 2
QKV_COLS = 6 * W_BRANCH
TQ = 256
TK = 256
N_QBLK = SEQ // TQ
TM = 512
FF_CHUNK = 256
VMEM_LIMIT = 56 * 1024 * 1024

F32 = jnp.float32
BF16 = jnp.bfloat16
NEG_BIG = -1e30


def _dot(a, b):
    return jnp.dot(a, b, preferred_element_type=F32)


def _dot_nt(a, b):
    return lax.dot_general(a, b, (((1,), (1,)), ((), ())), preferred_element_type=F32)


def _rms_scale(x):
    return lax.rsqrt(jnp.mean(x * x, axis=-1, keepdims=True) + RMS_EPS)


def _sigmoid(x):
    return 1.0 / (1.0 + jnp.exp(-x))


def _split3(x):
    hi = x.astype(BF16)
    r1 = x - hi.astype(F32)
    mid = r1.astype(BF16)
    lo = (r1 - mid.astype(F32)).astype(BF16)
    return hi, mid, lo


def _inproj_kernel(h_ref, g_ref, wqkv_ref, wg_ref, wf_ref, bf_ref, qkv_ref, gate_ref, lf_ref):
    x = h_ref[...]
    hn = ((x * _rms_scale(x)) * g_ref[...]).astype(BF16)
    for c in range(6):
        cols = slice(c * W_BRANCH, (c + 1) * W_BRANCH)
        y = _dot(hn, wqkv_ref[:, cols])
        if c in (0, 3):
            y = y * (HEAD_DIM ** -0.5)
        qkv_ref[:, cols] = y.astype(BF16)
    gate_ref[...] = _dot(hn, wg_ref[...])
    f = _dot(hn, wf_ref[...]) + bf_ref[...]
    lf_ref[...] = jnp.minimum(f, 0.0) - jnp.log(1.0 + jnp.exp(-jnp.abs(f)))


def _cumsum_kernel(lf_ref, init_ref, tri_ref, c_ref):
    tri = tri_ref[...]
    blk = tri.shape[0]
    carry = init_ref[...]
    for j in range(lf_ref.shape[0] // blk):
        rows = slice(j * blk, (j + 1) * blk)
        hi, mid, lo = _split3(lf_ref[rows, :])
        cs = (_dot(tri, hi) + _dot(tri, mid)) + _dot(tri, lo) + carry
        c_ref[rows, :] = cs
        carry = cs[blk - 1:blk, :]


def _head_queries(q2):
    q2 = q2.astype(F32)
    lane = lax.broadcasted_iota(jnp.int32, q2.shape, 1)
    first = lane < HEAD_DIM
    return (jnp.where(first, q2, 0.0).astype(BF16), jnp.where(first, 0.0, q2).astype(BF16))


def _merge_heads(acc0, acc1):
    lane = lax.broadcasted_iota(jnp.int32, acc0.shape, 1)
    return jnp.where(lane < HEAD_DIM, acc0, acc1)


def _sb_split(z, mask):
    nlk = jnp.maximum(z, 0.0) + jnp.log(1.0 + jnp.exp(-jnp.abs(z)))
    if mask is not None:
        nlk = jnp.where(mask, nlk, 0.0)
    hi = nlk.astype(BF16)
    return hi, (nlk - hi.astype(F32)).astype(BF16)


def _sb_weights(z, cl, carry, mask):
    w = jnp.exp(z - cl - carry)
    if mask is not None:
        w = jnp.where(mask, w, 0.0)
    return w.astype(BF16)


def _attn_tile(q_sb, q_fx, ks, vs, kf, vf, ct, cs, u, state, sb_mask, fx_mask):
    heads = range(2)
    z = [_dot_nt(q_sb[h], ks) for h in heads]
    s = [_dot_nt(q_fx[h], kf) + (ct[h] - cs[h]) for h in heads]
    parts = [_sb_split(z[h], sb_mask) for h in heads]
    if fx_mask is not None:
        s = [jnp.where(fx_mask, s[h], NEG_BIG) for h in heads]
    m_new = [jnp.maximum(state[2 + h][1], jnp.max(s[h], axis=1, keepdims=True)) for h in heads]
    alpha = [jnp.exp(state[2 + h][1] - m_new[h]) for h in heads]
    p = [jnp.exp(s[h] - m_new[h]) for h in heads]
    cl = [_dot(parts[h][0], u) + _dot(parts[h][1], u) for h in heads]
    pv_fx = [_dot(p[h].astype(BF16), vf) for h in heads]
    w = [_sb_weights(z[h], cl[h], state[h][1], sb_mask) for h in heads]
    pv_sb = [_dot(w[h], vs) for h in heads]
    new = [(state[h][0] + pv_sb[h], state[h][1] + cl[h][:, 0:1]) for h in heads]
    new += [(alpha[h] * state[2 + h][0] + pv_fx[h], m_new[h],
             alpha[h] * state[2 + h][2] + jnp.sum(p[h], axis=1, keepdims=True)) for h in heads]
    return tuple(new)


def _sb_init(rows):
    return (jnp.zeros((rows, LANES), F32), jnp.zeros((rows, 1), F32))


def _fox_init(rows):
    return (jnp.zeros((rows, LANES), F32), jnp.full((rows, 1), NEG_BIG, F32),
            jnp.zeros((rows, 1), F32))


def _attn_kernel(qs_ref, ks_ref, vs_ref, qf_ref, kf_ref, vf_ref, ccol_ref, crow_ref,
                 mks_ref, mvs_ref, mkf_ref, mvf_ref, mcrow_ref, u_ref, osb_ref, ofx_ref):
    u = u_ref[...]
    row = lax.broadcasted_iota(jnp.int32, (TQ, TK), 0)
    col = lax.broadcasted_iota(jnp.int32, (TQ, TK), 1)
    strict = col < row
    causal = col <= row
    meta_valid = lax.broadcasted_iota(jnp.int32, (TQ, META_ROWS), 1) < N_META
    u_meta = u[:META_ROWS, :META_ROWS]

    def q_block(qi, _):
        q0 = pl.multiple_of(qi * TQ, TQ)
        q_sb = _head_queries(qs_ref[pl.ds(q0, TQ), :])
        q_fx = _head_queries(qf_ref[pl.ds(q0, TQ), :])
        ct = [ccol_ref[0, 0, pl.ds(q0, TQ), h:h + 1] for h in range(2)]

        def tile(k0, state, sb_mask, fx_mask):
            cs = [crow_ref[0, 0, h:h + 1, pl.ds(k0, TK)] for h in range(2)]
            return _attn_tile(q_sb, q_fx, ks_ref[pl.ds(k0, TK), :], vs_ref[pl.ds(k0, TK), :],
                              kf_ref[pl.ds(k0, TK), :], vf_ref[pl.ds(k0, TK), :], ct, cs, u,
                              state, sb_mask, fx_mask)

        state = (_sb_init(TQ), _sb_init(TQ), _fox_init(TQ), _fox_init(TQ))
        state = tile(q0, state, strict, causal)
        state = lax.fori_loop(
            0, qi, lambda i, s: tile(pl.multiple_of((qi - 1 - i) * TK, TK), s, None, None), state)
        new = _attn_tile(q_sb, q_fx, mks_ref[...], mvs_ref[...], mkf_ref[...], mvf_ref[...], ct,
                         [mcrow_ref[0, h:h + 1, :] for h in range(2)], u_meta, state,
                         meta_valid, meta_valid)
        osb_ref[pl.ds(q0, TQ), :] = _merge_heads(new[0][0], new[1][0]).astype(BF16)
        ofx_ref[pl.ds(q0, TQ), :] = _merge_heads(new[2][0] / new[2][2],
                                                 new[3][0] / new[3][2]).astype(BF16)
        return 0

    lax.fori_loop(0, N_QBLK, q_block, 0)


def _meta_attn_kernel(qs_ref, ks_ref, vs_ref, qf_ref, kf_ref, vf_ref, ccol_ref, crow_ref, u_ref,
                      osb_ref, ofx_ref):
    row = lax.broadcasted_iota(jnp.int32, (META_ROWS, META_ROWS), 0)
    col = lax.broadcasted_iota(jnp.int32, (META_ROWS, META_ROWS), 1)
    state = (_sb_init(META_ROWS), _sb_init(META_ROWS), _fox_init(META_ROWS), _fox_init(META_ROWS))
    new = _attn_tile(_head_queries(qs_ref[...]), _head_queries(qf_ref[...]), ks_ref[...], vs_ref[...],
                     kf_ref[...], vf_ref[...], [ccol_ref[0, :, h:h + 1] for h in range(2)],
                     [crow_ref[0, h:h + 1, :] for h in range(2)], u_ref[...], state,
                     col < row, col <= row)
    osb_ref[...] = _merge_heads(new[0][0], new[1][0]).astype(BF16)
    ofx_ref[...] = _merge_heads(new[2][0] / new[2][2], new[3][0] / new[3][2]).astype(BF16)


def _outproj_kernel(h_ref, osb_ref, ofx_ref, gate_ref, wbs_ref, wbf_ref, wo_ref, g2_ref,
                    h1_ref, hn2_ref):
    t_sb = _dot(osb_ref[...], wbs_ref[...])
    t_fx = _dot(ofx_ref[...], wbf_ref[...])
    merged = (_sigmoid(gate_ref[:, :D_MODEL]) * t_sb + _sigmoid(gate_ref[:, D_MODEL:]) * t_fx)
    h1 = h_ref[...] + _dot(merged.astype(BF16), wo_ref[...])
    h1_ref[...] = h1
    hn2_ref[...] = ((h1 * _rms_scale(h1)) * g2_ref[...]).astype(BF16)


def _meta_up_kernel(hn2_ref, wup_ref, hist_ref):
    u = _dot(hn2_ref[0:N_META, :], wup_ref[...])
    hist_ref[...] = u[N_META - SUBLANES:, :]


def _ffn_kernel(hn2_ref, h1_ref, hist0_ref, wup_ref, cw_ref, wdn_ref, gf_ref, out_ref,
                ua_ref, ub_ref, hist_ref, acc_ref):
    @pl.when(pl.program_id(1) == 0)
    def _():
        hist_ref[...] = hist0_ref[...]

    hn2 = hn2_ref[...]
    tm = hn2.shape[0]
    for c in range(D_FF // FF_CHUNK):
        parts = []
        for half, u_ref in ((0, ua_ref), (1, ub_ref)):
            cols = slice(half * D_FF + c * FF_CHUNK, half * D_FF + (c + 1) * FF_CHUNK)
            u = _dot(hn2, wup_ref[:, cols])
            u_ref[0:SUBLANES, :] = hist_ref[:, cols]
            u_ref[SUBLANES:SUBLANES + tm, :] = u
            hist_ref[:, cols] = u[tm - SUBLANES:, :]
            cw = cw_ref[:, cols]
            parts.append(cw[0:1, :] * u_ref[SUBLANES - 2:SUBLANES - 2 + tm, :]
                         + cw[1:2, :] * u_ref[SUBLANES - 1:SUBLANES - 1 + tm, :]
                         + cw[2:3, :] * u)
        a, b = parts
        gated = (a * _sigmoid(a) * b).astype(BF16)
        contrib = _dot(gated, wdn_ref[c * FF_CHUNK:(c + 1) * FF_CHUNK, :])
        if c == 0:
            acc_ref[...] = contrib
        else:
            acc_ref[...] += contrib
    h2 = h1_ref[...] + acc_ref[...]
    out_ref[...] = (h2 * _rms_scale(h2)) * gf_ref[...]


def _const_spec(shape):
    return pl.BlockSpec(shape, lambda *_: (0,) * len(shape))


def _params(*semantics):
    return pltpu.CompilerParams(dimension_semantics=semantics, vmem_limit_bytes=VMEM_LIMIT)


def _inproj(h, tm, g, w_qkv, w_g, w_f, b_f, name):
    rows = h.shape[0]
    return pl.pallas_call(
        _inproj_kernel,
        grid=(rows // tm,),
        in_specs=[
            pl.BlockSpec((tm, D_MODEL), lambda i: (i, 0)),
            _const_spec((1, D_MODEL)),
            _const_spec((D_MODEL, QKV_COLS)),
            _const_spec((D_MODEL, 2 * D_MODEL)),
            _const_spec((D_MODEL, LANES)),
            _const_spec((1, LANES)),
        ],
        out_specs=[
            pl.BlockSpec((tm, QKV_COLS), lambda i: (i, 0)),
            pl.BlockSpec((tm, 2 * D_MODEL), lambda i: (i, 0)),
            pl.BlockSpec((tm, LANES), lambda i: (i, 0)),
        ],
        out_shape=[
            jax.ShapeDtypeStruct((rows, QKV_COLS), BF16),
            jax.ShapeDtypeStruct((rows, 2 * D_MODEL), F32),
            jax.ShapeDtypeStruct((rows, LANES), F32),
        ],
        compiler_params=_params("parallel"),
        name=name,
    )(h, g, w_qkv, w_g, w_f, b_f)


def _cumsum(lf, seq_rows, init, tri, name):
    rows = lf.shape[0]
    return pl.pallas_call(
        _cumsum_kernel,
        grid=(rows // seq_rows,),
        in_specs=[pl.BlockSpec((seq_rows, LANES), lambda b: (b, 0)), _const_spec((1, LANES)),
                  _const_spec(tri.shape)],
        out_specs=pl.BlockSpec((seq_rows, LANES), lambda b: (b, 0)),
        out_shape=jax.ShapeDtypeStruct((rows, LANES), F32),
        compiler_params=_params("parallel"),
        name=name,
    )(lf, init, tri)


def _outproj(h, tm, o_sb, o_fx, gate, wb_sb, wb_fx, wo, g2, name):
    rows = h.shape[0]
    return pl.pallas_call(
        _outproj_kernel,
        grid=(rows // tm,),
        in_specs=[
            pl.BlockSpec((tm, D_MODEL), lambda i: (i, 0)),
            pl.BlockSpec((tm, W_BRANCH), lambda i: (i, 0)),
            pl.BlockSpec((tm, W_BRANCH), lambda i: (i, 0)),
            pl.BlockSpec((tm, 2 * D_MODEL), lambda i: (i, 0)),
            _const_spec((W_BRANCH, D_MODEL)),
            _const_spec((W_BRANCH, D_MODEL)),
            _const_spec((D_MODEL, D_MODEL)),
            _const_spec((1, D_MODEL)),
        ],
        out_specs=[
            pl.BlockSpec((tm, D_MODEL), lambda i: (i, 0)),
            pl.BlockSpec((tm, D_MODEL), lambda i: (i, 0)),
        ],
        out_shape=[
            jax.ShapeDtypeStruct((rows, D_MODEL), F32),
            jax.ShapeDtypeStruct((rows, D_MODEL), BF16),
        ],
        compiler_params=_params("parallel"),
        name=name,
    )(h, o_sb, o_fx, gate, wb_sb, wb_fx, wo, g2)


def _pair_views(cum, batch, seq_rows):
    c = cum[:, :N_HEADS].reshape(batch, seq_rows, HEAD_PAIRS, 2)
    return c.transpose(0, 2, 1, 3), c.transpose(0, 2, 3, 1)


def kernel(x, meta_tokens, norm_mix_g, w_in, b_forget, w_branch_sb, w_branch_fox, w_out,
           norm_ffn_g, w_up, conv_w, w_down, norm_final_g):
    batch = x.shape[0]
    assert x.shape == (batch, SEQ, D_MODEL) and w_in.shape[0] == 1
    rows = batch * SEQ

    w_in0 = w_in[0]
    w_qkv = w_in0[:, :QKV_COLS].astype(BF16)
    w_f = jnp.pad(w_in0[:, QKV_COLS:QKV_COLS + N_HEADS], ((0, 0), (0, LANES - N_HEADS))).astype(BF16)
    w_g = w_in0[:, QKV_COLS + N_HEADS:].astype(BF16)
    b_f = jnp.pad(b_forget[0].astype(F32), (0, LANES - N_HEADS)).reshape(1, LANES)
    g1 = norm_mix_g[0].reshape(1, D_MODEL)
    g2 = norm_ffn_g[0].reshape(1, D_MODEL)
    wb_sb, wb_fx, wo = (w_branch_sb[0].astype(BF16), w_branch_fox[0].astype(BF16),
                        w_out[0].astype(BF16))
    w_up_b, w_down_b = w_up[0].astype(BF16), w_down[0].astype(BF16)

    idx = jnp.arange(TK)
    u_mat = (idx[:, None] >= idx[None, :]).astype(BF16)
    tri = u_mat[:META_ROWS, :META_ROWS]
    h_real = x.reshape(rows, D_MODEL)
    h_meta = jnp.pad(meta_tokens.astype(x.dtype), ((0, META_ROWS - N_META), (0, 0)))

    qkv_m, gate_m, lf_m = _inproj(h_meta, META_ROWS, g1, w_qkv, w_g, w_f, b_f, "inproj_meta")
    cum_m = _cumsum(lf_m, META_ROWS, jnp.zeros((1, LANES), F32), tri, "forget_cumsum_meta")
    mc_col, mc_row = _pair_views(cum_m, 1, META_ROWS)
    mc_col, mc_row = mc_col[0], mc_row[0]

    def meta_blk(base):
        return pl.BlockSpec((META_ROWS, LANES), lambda p: (0, base + p))

    o_sb_m, o_fx_m = pl.pallas_call(
        _meta_attn_kernel,
        grid=(HEAD_PAIRS,),
        in_specs=[meta_blk(c * HEAD_PAIRS) for c in range(6)] + [
            pl.BlockSpec((1, META_ROWS, 2), lambda p: (p, 0, 0)),
            pl.BlockSpec((1, 2, META_ROWS), lambda p: (p, 0, 0)),
            _const_spec((META_ROWS, META_ROWS))],
        out_specs=[pl.BlockSpec((META_ROWS, LANES), lambda p: (0, p))] * 2,
        out_shape=[jax.ShapeDtypeStruct((META_ROWS, W_BRANCH), BF16)] * 2,
        compiler_params=_params("parallel"),
        name="attn_meta",
    )(qkv_m, qkv_m, qkv_m, qkv_m, qkv_m, qkv_m, mc_col, mc_row, u_mat[:META_ROWS, :META_ROWS])

    qkv, gate, lf = _inproj(h_real, TM, g1, w_qkv, w_g, w_f, b_f, "inproj")
    cum = _cumsum(lf, SEQ, cum_m[N_META - 1:N_META, :], tri, "forget_cumsum")
    c_col, c_row = _pair_views(cum, batch, SEQ)

    def seq_blk(base):
        return pl.BlockSpec((SEQ, LANES), lambda b, p: (b, base + p))

    def meta_kv(base):
        return pl.BlockSpec((META_ROWS, LANES), lambda b, p: (0, base + p))

    o_sb, o_fx = pl.pallas_call(
        _attn_kernel,
        grid=(batch, HEAD_PAIRS),
        in_specs=[seq_blk(c * HEAD_PAIRS) for c in range(6)] + [
            pl.BlockSpec((1, 1, SEQ, 2), lambda b, p: (b, p, 0, 0)),
            pl.BlockSpec((1, 1, 2, SEQ), lambda b, p: (b, p, 0, 0)),
            meta_kv(1 * HEAD_PAIRS), meta_kv(2 * HEAD_PAIRS),
            meta_kv(4 * HEAD_PAIRS), meta_kv(5 * HEAD_PAIRS),
            pl.BlockSpec((1, 2, META_ROWS), lambda b, p: (p, 0, 0)),
            pl.BlockSpec((TK, TK), lambda b, p: (0, 0))],
        out_specs=[pl.BlockSpec((SEQ, LANES), lambda b, p: (b, p))] * 2,
        out_shape=[jax.ShapeDtypeStruct((rows, W_BRANCH), BF16)] * 2,
        compiler_params=_params("parallel", "parallel"),
        name="attn",
    )(qkv, qkv, qkv, qkv, qkv, qkv, c_col, c_row, qkv_m, qkv_m, qkv_m, qkv_m, mc_row, u_mat)

    _, hn2_m = _outproj(h_meta, META_ROWS, o_sb_m, o_fx_m, gate_m, wb_sb, wb_fx, wo, g2,
                        "outproj_meta")
    h1, hn2 = _outproj(h_real, TM, o_sb, o_fx, gate, wb_sb, wb_fx, wo, g2, "outproj")

    up_cols = 512
    hist0 = pl.pallas_call(
        _meta_up_kernel,
        grid=(2 * D_FF // up_cols,),
        in_specs=[_const_spec((META_ROWS, D_MODEL)),
                  pl.BlockSpec((D_MODEL, up_cols), lambda c: (0, c))],
        out_specs=pl.BlockSpec((SUBLANES, up_cols), lambda c: (0, c)),
        out_shape=jax.ShapeDtypeStruct((SUBLANES, 2 * D_FF), F32),
        compiler_params=_params("parallel"),
        name="up_meta",
    )(hn2_m, w_up_b)

    tiles = SEQ // TM
    out = pl.pallas_call(
        _ffn_kernel,
        grid=(batch, tiles),
        in_specs=[
            pl.BlockSpec((TM, D_MODEL), lambda b, t: (b * tiles + t, 0)),
            pl.BlockSpec((TM, D_MODEL), lambda b, t: (b * tiles + t, 0)),
            _const_spec((SUBLANES, 2 * D_FF)),
            pl.BlockSpec((D_MODEL, 2 * D_FF), lambda b, t: (0, 0), pipeline_mode=pl.Buffered(1)),
            _const_spec((CONV_W, 2 * D_FF)),
            pl.BlockSpec((D_FF, D_MODEL), lambda b, t: (0, 0), pipeline_mode=pl.Buffered(1)),
            _const_spec((1, D_MODEL)),
        ],
        out_specs=pl.BlockSpec((TM, D_MODEL), lambda b, t: (b * tiles + t, 0)),
        out_shape=jax.ShapeDtypeStruct((rows, D_MODEL), F32),
        scratch_shapes=[
            pltpu.VMEM((SUBLANES + TM, FF_CHUNK), F32),
            pltpu.VMEM((SUBLANES + TM, FF_CHUNK), F32),
            pltpu.VMEM((SUBLANES, 2 * D_FF), F32),
            pltpu.VMEM((TM, D_MODEL), F32),
        ],
        compiler_params=_params("arbitrary", "arbitrary"),
        name="conv_ffn",
    )(hn2, h1, hist0, w_up_b, conv_w[0].astype(F32), w_down_b, norm_final_g.reshape(1, D_MODEL))

    return out.reshape(batch, SEQ, D_MODEL)
```

```python
import jax
import jax.numpy as jnp
from jax import lax
from jax.experimental import pallas as pl
from jax.experimental.pallas import tpu as pltpu

D_MODEL = 1024
SEQ = 2048
N_META = 16
HEAD_DIM = 64
N_HEADS = 8
W_BRANCH = N_HEADS * HEAD_DIM
D_FF = 2816
CONV_W = 3
RMS_EPS = 1e-6

LANES = 128
SUBLANES = 8
META_ROWS = 128
HEAD_PAIRS = N_HEADS // 2
QKV_COLS = 6 * W_BRANCH
TQ = 256
TK = 256
N_QBLK = SEQ // TQ
TM = 512
FF_CHUNK = 256
VMEM_LIMIT = 56 * 1024 * 1024

F32 = jnp.float32
BF16 = jnp.bfloat16
NEG_BIG = -1e30
LOG2E = 1.4426950408889634


def _dot(a, b):
    return jnp.dot(a, b, preferred_element_type=F32)


def _dot_nt(a, b):
    return lax.dot_general(a, b, (((1,), (1,)), ((), ())), preferred_element_type=F32)


def _rms_scale(x):
    return lax.rsqrt(jnp.mean(x * x, axis=-1, keepdims=True) + RMS_EPS)


def _sigmoid(x):
    return 1.0 / (1.0 + jnp.exp(-x))


def _split3(x):
    hi = x.astype(BF16)
    r1 = x - hi.astype(F32)
    mid = r1.astype(BF16)
    lo = (r1 - mid.astype(F32)).astype(BF16)
    return hi, mid, lo


def _inproj_kernel(h_ref, g_ref, wqkv_ref, wg_ref, wf_ref, bf_ref, qkv_ref, gate_ref, lf_ref):
    x = h_ref[...]
    hn = ((x * _rms_scale(x)) * g_ref[...]).astype(BF16)
    for c in range(6):
        cols = slice(c * W_BRANCH, (c + 1) * W_BRANCH)
        y = _dot(hn, wqkv_ref[:, cols])
        if c in (0, 3):
            y = y * (HEAD_DIM ** -0.5 * LOG2E)
        qkv_ref[:, cols] = y.astype(BF16)
    gate_ref[...] = _dot(hn, wg_ref[...]).astype(BF16)
    f = _dot(hn, wf_ref[...]) + bf_ref[...]
    lf_ref[...] = (jnp.minimum(f, 0.0) - jnp.log(1.0 + jnp.exp(-jnp.abs(f)))) * LOG2E


def _cumsum_kernel(lf_ref, init_ref, tri_ref, c_ref):
    tri = tri_ref[...]
    blk = tri.shape[0]
    carry = init_ref[...]
    for j in range(lf_ref.shape[0] // blk):
        rows = slice(j * blk, (j + 1) * blk)
        hi, mid, lo = _split3(lf_ref[rows, :])
        cs = (_dot(tri, hi) + _dot(tri, mid)) + _dot(tri, lo) + carry
        c_ref[rows, :] = cs
        carry = cs[blk - 1:blk, :]


def _head_queries(q2):
    q2 = q2.astype(F32)
    lane = lax.broadcasted_iota(jnp.int32, q2.shape, 1)
    first = lane < HEAD_DIM
    return (jnp.where(first, q2, 0.0).astype(BF16), jnp.where(first, 0.0, q2).astype(BF16))


def _merge_heads(acc0, acc1):
    lane = lax.broadcasted_iota(jnp.int32, acc0.shape, 1)
    return jnp.where(lane < HEAD_DIM, acc0, acc1)


def _softplus(z, mask):
    nlk = jnp.maximum(z, 0.0) + jnp.log(1.0 + jnp.exp2(-jnp.abs(z))) * LOG2E
    if mask is not None:
        nlk = jnp.where(mask, nlk, 0.0)
    return nlk


def _sb_weights(log_beta, cl, carry, mask):
    w = jnp.exp2(log_beta - cl - carry)
    if mask is not None:
        w = jnp.where(mask, w, 0.0)
    return w.astype(BF16)


def _attn_tile(q_sb, q_fx, ks, vs, kf, vf, ct, cs, u, state, sb_mask, fx_mask, acc_ref):
    heads = range(2)
    z = [_dot_nt(q_sb[h], ks) for h in heads]
    s = [_dot_nt(q_fx[h], kf) - cs[h] for h in heads]
    nlk = [_softplus(z[h], sb_mask) for h in heads]
    if fx_mask is not None:
        s = [jnp.where(fx_mask, s[h], NEG_BIG) for h in heads]
    m_new = [jnp.maximum(state[2 + h][0], jnp.max(s[h], axis=1, keepdims=True) + ct[h])
             for h in heads]
    alpha = [jnp.exp2(state[2 + h][0] - m_new[h]) for h in heads]
    p = [jnp.exp2(s[h] + (ct[h] - m_new[h])) for h in heads]
    cl = [_dot(nlk[h].astype(BF16), u) for h in heads]
    pv_fx = [_dot(p[h].astype(BF16), vf) for h in heads]
    w = [_sb_weights(z[h] - nlk[h], cl[h], state[h][0], sb_mask) for h in heads]
    pv_sb = [_dot(w[h], vs) for h in heads]
    for h in heads:
        acc_ref[h] = acc_ref[h] + pv_sb[h]
        acc_ref[2 + h] = alpha[h] * acc_ref[2 + h] + pv_fx[h]
    new = [(state[h][0] + (cl[h][:, 0:1] + nlk[h][:, 0:1]),) for h in heads]
    new += [(m_new[h], alpha[h] * state[2 + h][1] + jnp.sum(p[h], axis=1, keepdims=True))
            for h in heads]
    return tuple(new)


def _sb_init(rows):
    return (jnp.zeros((rows, 1), F32),)


def _fox_init(rows):
    return (jnp.full((rows, 1), NEG_BIG, F32), jnp.zeros((rows, 1), F32))


def _attn_kernel(qs_ref, ks_ref, vs_ref, qf_ref, kf_ref, vf_ref, ccol_ref, crow_ref,
                 mks_ref, mvs_ref, mkf_ref, mvf_ref, mcrow_ref, u_ref, osb_ref, ofx_ref, acc_ref):
    u = u_ref[...]
    row = lax.broadcasted_iota(jnp.int32, (TQ, TK), 0)
    col = lax.broadcasted_iota(jnp.int32, (TQ, TK), 1)
    strict = col < row
    causal = col <= row
    meta_valid = lax.broadcasted_iota(jnp.int32, (TQ, META_ROWS), 1) < N_META
    u_meta = u[:META_ROWS, :META_ROWS]

    def q_block(qi, _):
        q0 = pl.multiple_of(qi * TQ, TQ)
        q_sb = _head_queries(qs_ref[pl.ds(q0, TQ), :])
        q_fx = _head_queries(qf_ref[pl.ds(q0, TQ), :])
        ct = [ccol_ref[0, 0, pl.ds(q0, TQ), h:h + 1] for h in range(2)]

        def tile(k0, state, sb_mask, fx_mask):
            cs = [crow_ref[0, 0, h:h + 1, pl.ds(k0, TK)] for h in range(2)]
            return _attn_tile(q_sb, q_fx, ks_ref[pl.ds(k0, TK), :], vs_ref[pl.ds(k0, TK), :],
                              kf_ref[pl.ds(k0, TK), :], vf_ref[pl.ds(k0, TK), :], ct, cs, u,
                              state, sb_mask, fx_mask, acc_ref)

        acc_ref[...] = jnp.zeros(acc_ref.shape, F32)
        state = (_sb_init(TQ), _sb_init(TQ), _fox_init(TQ), _fox_init(TQ))
        state = tile(q0, state, strict, causal)
        state = lax.fori_loop(
            0, qi, lambda i, s: tile(pl.multiple_of((qi - 1 - i) * TK, TK), s, None, None), state)
        new = _attn_tile(q_sb, q_fx, mks_ref[...], mvs_ref[...], mkf_ref[...], mvf_ref[...], ct,
                         [mcrow_ref[0, h:h + 1, :] for h in range(2)], u_meta, state,
                         meta_valid, meta_valid, acc_ref)
        osb_ref[pl.ds(q0, TQ), :] = _merge_heads(acc_ref[0], acc_ref[1]).astype(BF16)
        ofx_ref[pl.ds(q0, TQ), :] = _merge_heads(acc_ref[2] / new[2][1],
                                                 acc_ref[3] / new[3][1]).astype(BF16)
        return 0

    lax.fori_loop(0, N_QBLK, q_block, 0)


def _meta_attn_kernel(qs_ref, ks_ref, vs_ref, qf_ref, kf_ref, vf_ref, ccol_ref, crow_ref, u_ref,
                      osb_ref, ofx_ref, acc_ref):
    row = lax.broadcasted_iota(jnp.int32, (META_ROWS, META_ROWS), 0)
    col = lax.broadcasted_iota(jnp.int32, (META_ROWS, META_ROWS), 1)
    acc_ref[...] = jnp.zeros(acc_ref.shape, F32)
    state = (_sb_init(META_ROWS), _sb_init(META_ROWS), _fox_init(META_ROWS), _fox_init(META_ROWS))
    new = _attn_tile(_head_queries(qs_ref[...]), _head_queries(qf_ref[...]), ks_ref[...], vs_ref[...],
                     kf_ref[...], vf_ref[...], [ccol_ref[0, :, h:h + 1] for h in range(2)],
                     [crow_ref[0, h:h + 1, :] for h in range(2)], u_ref[...], state,
                     col < row, col <= row, acc_ref)
    osb_ref[...] = _merge_heads(acc_ref[0], acc_ref[1]).astype(BF16)
    ofx_ref[...] = _merge_heads(acc_ref[2] / new[2][1], acc_ref[3] / new[3][1]).astype(BF16)


def _outproj_kernel(h_ref, osb_ref, ofx_ref, gate_ref, wbs_ref, wbf_ref, wo_ref, g2_ref,
                    h1_ref, hn2_ref):
    t_sb = _dot(osb_ref[...], wbs_ref[...])
    t_fx = _dot(ofx_ref[...], wbf_ref[...])
    merged = (_sigmoid(gate_ref[:, :D_MODEL].astype(F32)) * t_sb
              + _sigmoid(gate_ref[:, D_MODEL:].astype(F32)) * t_fx)
    h1 = h_ref[...] + _dot(merged.astype(BF16), wo_ref[...])
    h1_ref[...] = h1
    hn2_ref[...] = ((h1 * _rms_scale(h1)) * g2_ref[...]).astype(BF16)


def _meta_up_kernel(hn2_ref, wup_ref, hist_ref):
    u = _dot(hn2_ref[0:N_META, :], wup_ref[...])
    hist_ref[...] = u[N_META - SUBLANES:, :]


def _ffn_kernel(hn2_ref, h1_ref, hist0_ref, wup_ref, cw_ref, wdn_ref, gf_ref, out_ref,
                u_ref, hist_ref, acc_ref):
    @pl.when(pl.program_id(1) == 0)
    def _():
        hist_ref[...] = hist0_ref[...]

    hn2 = hn2_ref[...]
    tm = hn2.shape[0]
    n_chunks = D_FF // FF_CHUNK
    u_ref[0:SUBLANES, :] = hist_ref[...]

    def chunk_cols(c):
        return [slice(half * D_FF + c * FF_CHUNK, half * D_FF + (c + 1) * FF_CHUNK)
                for half in range(2)]

    def up(c):
        for cols in chunk_cols(c):
            u_ref[SUBLANES:SUBLANES + tm, cols] = _dot(hn2, wup_ref[:, cols])

    def gate_down(c):
        a, b = [sum(cw_ref[j:j + 1, cols] * u_ref[SUBLANES - 2 + j:SUBLANES - 2 + j + tm, cols]
                    for j in range(CONV_W)) for cols in chunk_cols(c)]
        gated = (a * _sigmoid(a) * b).astype(BF16)
        contrib = _dot(gated, wdn_ref[c * FF_CHUNK:(c + 1) * FF_CHUNK, :])
        if c == 0:
            acc_ref[...] = contrib
        else:
            acc_ref[...] += contrib

    up(0)
    for c in range(n_chunks):
        if c + 1 < n_chunks:
            up(c + 1)
        gate_down(c)
    hist_ref[...] = u_ref[tm:tm + SUBLANES, :]
    h2 = h1_ref[...] + acc_ref[...]
    out_ref[...] = (h2 * _rms_scale(h2)) * gf_ref[...]


def _const_spec(shape):
    return pl.BlockSpec(shape, lambda *_: (0,) * len(shape))


def _params(*semantics):
    return pltpu.CompilerParams(dimension_semantics=semantics, vmem_limit_bytes=VMEM_LIMIT)


def _inproj(h, tm, g, w_qkv, w_g, w_f, b_f, name):
    rows = h.shape[0]
    return pl.pallas_call(
        _inproj_kernel,
        grid=(rows // tm,),
        in_specs=[
            pl.BlockSpec((tm, D_MODEL), lambda i: (i, 0)),
            _const_spec((1, D_MODEL)),
            _const_spec((D_MODEL, QKV_COLS)),
            _const_spec((D_MODEL, 2 * D_MODEL)),
            _const_spec((D_MODEL, LANES)),
            _const_spec((1, LANES)),
        ],
        out_specs=[
            pl.BlockSpec((tm, QKV_COLS), lambda i: (i, 0)),
            pl.BlockSpec((tm, 2 * D_MODEL), lambda i: (i, 0)),
            pl.BlockSpec((tm, LANES), lambda i: (i, 0)),
        ],
        out_shape=[
            jax.ShapeDtypeStruct((rows, QKV_COLS), BF16),
            jax.ShapeDtypeStruct((rows, 2 * D_MODEL), BF16),
            jax.ShapeDtypeStruct((rows, LANES), F32),
        ],
        compiler_params=_params("parallel"),
        name=name,
    )(h, g, w_qkv, w_g, w_f, b_f)


def _cumsum(lf, seq_rows, init, tri, name):
    rows = lf.shape[0]
    return pl.pallas_call(
        _cumsum_kernel,
        grid=(rows // seq_rows,),
        in_specs=[pl.BlockSpec((seq_rows, LANES), lambda b: (b, 0)), _const_spec((1, LANES)),
                  _const_spec(tri.shape)],
        out_specs=pl.BlockSpec((seq_rows, LANES), lambda b: (b, 0)),
        out_shape=jax.ShapeDtypeStruct((rows, LANES), F32),
        compiler_params=_params("parallel"),
        name=name,
    )(lf, init, tri)


def _outproj(h, tm, o_sb, o_fx, gate, wb_sb, wb_fx, wo, g2, name):
    rows = h.shape[0]
    return pl.pallas_call(
        _outproj_kernel,
        grid=(rows // tm,),
        in_specs=[
            pl.BlockSpec((tm, D_MODEL), lambda i: (i, 0)),
            pl.BlockSpec((tm, W_BRANCH), lambda i: (i, 0)),
            pl.BlockSpec((tm, W_BRANCH), lambda i: (i, 0)),
            pl.BlockSpec((tm, 2 * D_MODEL), lambda i: (i, 0)),
            _const_spec((W_BRANCH, D_MODEL)),
            _const_spec((W_BRANCH, D_MODEL)),
            _const_spec((D_MODEL, D_MODEL)),
            _const_spec((1, D_MODEL)),
        ],
        out_specs=[
            pl.BlockSpec((tm, D_MODEL), lambda i: (i, 0)),
            pl.BlockSpec((tm, D_MODEL), lambda i: (i, 0)),
        ],
        out_shape=[
            jax.ShapeDtypeStruct((rows, D_MODEL), F32),
            jax.ShapeDtypeStruct((rows, D_MODEL), BF16),
        ],
        compiler_params=_params("parallel"),
        name=name,
    )(h, o_sb, o_fx, gate, wb_sb, wb_fx, wo, g2)


def _pair_views(cum, batch, seq_rows):
    c = cum[:, :N_HEADS].reshape(batch, seq_rows, HEAD_PAIRS, 2)
    return c.transpose(0, 2, 1, 3), c.transpose(0, 2, 3, 1)


def kernel(x, meta_tokens, norm_mix_g, w_in, b_forget, w_branch_sb, w_branch_fox, w_out,
           norm_ffn_g, w_up, conv_w, w_down, norm_final_g):
    batch = x.shape[0]
    assert x.shape == (batch, SEQ, D_MODEL) and w_in.shape[0] == 1
    rows = batch * SEQ

    w_in0 = w_in[0]
    w_qkv = w_in0[:, :QKV_COLS].astype(BF16)
    w_f = jnp.pad(w_in0[:, QKV_COLS:QKV_COLS + N_HEADS], ((0, 0), (0, LANES - N_HEADS))).astype(BF16)
    w_g = w_in0[:, QKV_COLS + N_HEADS:].astype(BF16)
    b_f = jnp.pad(b_forget[0].astype(F32), (0, LANES - N_HEADS)).reshape(1, LANES)
    g1 = norm_mix_g[0].reshape(1, D_MODEL)
    g2 = norm_ffn_g[0].reshape(1, D_MODEL)
    wb_sb, wb_fx, wo = (w_branch_sb[0].astype(BF16), w_branch_fox[0].astype(BF16),
                        w_out[0].astype(BF16))
    w_up_b, w_down_b = w_up[0].astype(BF16), w_down[0].astype(BF16)

    idx = jnp.arange(TK)
    u_mat = (idx[:, None] > idx[None, :]).astype(BF16)
    mi = jnp.arange(META_ROWS)
    tri = (mi[None, :] <= mi[:, None]).astype(BF16)
    h_real = x.reshape(rows, D_MODEL)
    h_meta = jnp.pad(meta_tokens.astype(x.dtype), ((0, META_ROWS - N_META), (0, 0)))

    qkv_m, gate_m, lf_m = _inproj(h_meta, META_ROWS, g1, w_qkv, w_g, w_f, b_f, "inproj_meta")
    cum_m = _cumsum(lf_m, META_ROWS, jnp.zeros((1, LANES), F32), tri, "forget_cumsum_meta")
    mc_col, mc_row = _pair_views(cum_m, 1, META_ROWS)
    mc_col, mc_row = mc_col[0], mc_row[0]

    def meta_blk(base):
        return pl.BlockSpec((META_ROWS, LANES), lambda p: (0, base + p))

    o_sb_m, o_fx_m = pl.pallas_call(
        _meta_attn_kernel,
        grid=(HEAD_PAIRS,),
        in_specs=[meta_blk(c * HEAD_PAIRS) for c in range(6)] + [
            pl.BlockSpec((1, META_ROWS, 2), lambda p: (p, 0, 0)),
            pl.BlockSpec((1, 2, META_ROWS), lambda p: (p, 0, 0)),
            _const_spec((META_ROWS, META_ROWS))],
        out_specs=[pl.BlockSpec((META_ROWS, LANES), lambda p: (0, p))] * 2,
        out_shape=[jax.ShapeDtypeStruct((META_ROWS, W_BRANCH), BF16)] * 2,
        scratch_shapes=[pltpu.VMEM((4, META_ROWS, LANES), F32)],
        compiler_params=_params("parallel"),
        name="attn_meta",
    )(qkv_m, qkv_m, qkv_m, qkv_m, qkv_m, qkv_m, mc_col, mc_row, u_mat[:META_ROWS, :META_ROWS])

    qkv, gate, lf = _inproj(h_real, TM, g1, w_qkv, w_g, w_f, b_f, "inproj")
    cum = _cumsum(lf, SEQ, cum_m[N_META - 1:N_META, :], tri, "forget_cumsum")
    c_col, c_row = _pair_views(cum, batch, SEQ)

    def seq_blk(base):
        return pl.BlockSpec((SEQ, LANES), lambda b, p: (b, base + p))

    def meta_kv(base):
        return pl.BlockSpec((META_ROWS, LANES), lambda b, p: (0, base + p))

    o_sb, o_fx = pl.pallas_call(
        _attn_kernel,
        grid=(batch, HEAD_PAIRS),
        in_specs=[seq_blk(c * HEAD_PAIRS) for c in range(6)] + [
            pl.BlockSpec((1, 1, SEQ, 2), lambda b, p: (b, p, 0, 0)),
            pl.BlockSpec((1, 1, 2, SEQ), lambda b, p: (b, p, 0, 0)),
            meta_kv(1 * HEAD_PAIRS), meta_kv(2 * HEAD_PAIRS),
            meta_kv(4 * HEAD_PAIRS), meta_kv(5 * HEAD_PAIRS),
            pl.BlockSpec((1, 2, META_ROWS), lambda b, p: (p, 0, 0)),
            pl.BlockSpec((TK, TK), lambda b, p: (0, 0))],
        out_specs=[pl.BlockSpec((SEQ, LANES), lambda b, p: (b, p))] * 2,
        out_shape=[jax.ShapeDtypeStruct((rows, W_BRANCH), BF16)] * 2,
        scratch_shapes=[pltpu.VMEM((4, TQ, LANES), F32)],
        compiler_params=_params("parallel", "parallel"),
        name="attn",
    )(qkv, qkv, qkv, qkv, qkv, qkv, c_col, c_row, qkv_m, qkv_m, qkv_m, qkv_m, mc_row, u_mat)

    _, hn2_m = _outproj(h_meta, META_ROWS, o_sb_m, o_fx_m, gate_m, wb_sb, wb_fx, wo, g2,
                        "outproj_meta")
    h1, hn2 = _outproj(h_real, TM, o_sb, o_fx, gate, wb_sb, wb_fx, wo, g2, "outproj")

    up_cols = 512
    hist0 = pl.pallas_call(
        _meta_up_kernel,
        grid=(2 * D_FF // up_cols,),
        in_specs=[_const_spec((META_ROWS, D_MODEL)),
                  pl.BlockSpec((D_MODEL, up_cols), lambda c: (0, c))],
        out_specs=pl.BlockSpec((SUBLANES, up_cols), lambda c: (0, c)),
        out_shape=jax.ShapeDtypeStruct((SUBLANES, 2 * D_FF), F32),
        compiler_params=_params("parallel"),
        name="up_meta",
    )(hn2_m, w_up_b)

    tiles = SEQ // TM
    out = pl.pallas_call(
        _ffn_kernel,
        grid=(batch, tiles),
        in_specs=[
            pl.BlockSpec((TM, D_MODEL), lambda b, t: (b * tiles + t, 0)),
            pl.BlockSpec((TM, D_MODEL), lambda b, t: (b * tiles + t, 0)),
            _const_spec((SUBLANES, 2 * D_FF)),
            pl.BlockSpec((D_MODEL, 2 * D_FF), lambda b, t: (0, 0), pipeline_mode=pl.Buffered(1)),
            _const_spec((CONV_W, 2 * D_FF)),
            pl.BlockSpec((D_FF, D_MODEL), lambda b, t: (0, 0), pipeline_mode=pl.Buffered(1)),
            _const_spec((1, D_MODEL)),
        ],
        out_specs=pl.BlockSpec((TM, D_MODEL), lambda b, t: (b * tiles + t, 0)),
        out_shape=jax.ShapeDtypeStruct((rows, D_MODEL), F32),
        scratch_shapes=[
            pltpu.VMEM((SUBLANES + TM, 2 * D_FF), F32),
            pltpu.VMEM((SUBLANES, 2 * D_FF), F32),
            pltpu.VMEM((TM, D_MODEL), F32),
        ],
        compiler_params=_params("arbitrary", "arbitrary"),
        name="conv_ffn",
    )(hn2, h1, hist0, w_up_b, conv_w[0].astype(F32), w_down_b, norm_final_g.reshape(1, D_MODEL))

    return out.reshape(batch, SEQ, D_MODEL)
```

```python
import jax
import jax.numpy as jnp
from jax import lax
from jax.experimental import pallas as pl
from jax.experimental.pallas import tpu as pltpu

D_MODEL = 1024
SEQ = 2048
N_META = 16
HEAD_DIM = 64
N_HEADS = 8
W_BRANCH = N_HEADS * HEAD_DIM
D_FF = 2816
CONV_W = 3
RMS_EPS = 1e-6

LANES = 128
SUBLANES = 8
META_ROWS = 128
HEAD_PAIRS = N_HEADS // 2
QKV_COLS = 6 * W_BRANCH
TQ = 256
TK = 256
N_QBLK = SEQ // TQ
TM = 512
FF_CHUNK = 256
VMEM_LIMIT = 56 * 1024 * 1024

F32 = jnp.float32
BF16 = jnp.bfloat16
NEG_BIG = -1e30


def _dot(a, b):
    return jnp.dot(a, b, preferred_element_type=F32)


def _dot_nt(a, b):
    return lax.dot_general(a, b, (((1,), (1,)), ((), ())), preferred_element_type=F32)


def _rms_scale(x):
    return lax.rsqrt(jnp.mean(x * x, axis=-1, keepdims=True) + RMS_EPS)


def _sigmoid(x):
    return 1.0 / (1.0 + jnp.exp(-x))


def _split3(x):
    hi = x.astype(BF16)
    r1 = x - hi.astype(F32)
    mid = r1.astype(BF16)
    lo = (r1 - mid.astype(F32)).astype(BF16)
    return hi, mid, lo


def _inproj_kernel(h_ref, g_ref, wqkv_ref, wg_ref, wf_ref, bf_ref, qkv_ref, gate_ref, lf_ref):
    x = h_ref[...]
    hn = ((x * _rms_scale(x)) * g_ref[...]).astype(BF16)
    for c in range(6):
        cols = slice(c * W_BRANCH, (c + 1) * W_BRANCH)
        y = _dot(hn, wqkv_ref[:, cols])
        if c in (0, 3):
            y = y * (HEAD_DIM ** -0.5)
        qkv_ref[:, cols] = y.astype(BF16)
    gate_ref[...] = _dot(hn, wg_ref[...]).astype(BF16)
    f = _dot(hn, wf_ref[...]) + bf_ref[...]
    lf_ref[...] = jnp.minimum(f, 0.0) - jnp.log(1.0 + jnp.exp(-jnp.abs(f)))


def _cumsum_kernel(lf_ref, init_ref, tri_ref, c_ref):
    tri = tri_ref[...]
    blk = tri.shape[0]
    carry = init_ref[...]
    for j in range(lf_ref.shape[0] // blk):
        rows = slice(j * blk, (j + 1) * blk)
        hi, mid, lo = _split3(lf_ref[rows, :])
        cs = (_dot(tri, hi) + _dot(tri, mid)) + _dot(tri, lo) + carry
        c_ref[rows, :] = cs
        carry = cs[blk - 1:blk, :]


def _head_queries(q2):
    q2 = q2.astype(F32)
    lane = lax.broadcasted_iota(jnp.int32, q2.shape, 1)
    first = lane < HEAD_DIM
    return (jnp.where(first, q2, 0.0).astype(BF16), jnp.where(first, 0.0, q2).astype(BF16))


def _merge_heads(acc0, acc1):
    lane = lax.broadcasted_iota(jnp.int32, acc0.shape, 1)
    return jnp.where(lane < HEAD_DIM, acc0, acc1)


def _sb_split(z, mask):
    nlk = jnp.maximum(z, 0.0) + jnp.log(1.0 + jnp.exp(-jnp.abs(z)))
    if mask is not None:
        nlk = jnp.where(mask, nlk, 0.0)
    hi = nlk.astype(BF16)
    return hi, (nlk - hi.astype(F32)).astype(BF16)


def _sb_weights(z, cl, carry, mask):
    w = jnp.exp(z - cl - carry)
    if mask is not None:
        w = jnp.where(mask, w, 0.0)
    return w.astype(BF16)


def _attn_tile(q_sb, q_fx, ks, vs, kf, vf, ct, cs, u, state, sb_mask, fx_mask):
    heads = range(2)
    z = [_dot_nt(q_sb[h], ks) for h in heads]
    s = [_dot_nt(q_fx[h], kf) + (ct[h] - cs[h]) for h in heads]
    parts = [_sb_split(z[h], sb_mask) for h in heads]
    if fx_mask is not None:
        s = [jnp.where(fx_mask, s[h], NEG_BIG) for h in heads]
    m_new = [jnp.maximum(state[2 + h][1], jnp.max(s[h], axis=1, keepdims=True)) for h in heads]
    alpha = [jnp.exp(state[2 + h][1] - m_new[h]) for h in heads]
    p = [jnp.exp(s[h] - m_new[h]) for h in heads]
    cl = [_dot(parts[h][0], u) + _dot(parts[h][1], u) for h in heads]
    pv_fx = [_dot(p[h].astype(BF16), vf) for h in heads]
    w = [_sb_weights(z[h], cl[h], state[h][1], sb_mask) for h in heads]
    pv_sb = [_dot(w[h], vs) for h in heads]
    new = [(state[h][0] + pv_sb[h], state[h][1] + cl[h][:, 0:1]) for h in heads]
    new += [(alpha[h] * state[2 + h][0] + pv_fx[h], m_new[h],
             alpha[h] * state[2 + h][2] + jnp.sum(p[h], axis=1, keepdims=True)) for h in heads]
    return tuple(new)


def _first_tile(q_sb, q_fx, ks, vs, kf, vf, mk_s, mv_s, mk_f, mv_f, ct, cs, mcs, u, u_meta,
                strict, causal, meta_valid, meta_ref):
    heads = range(2)
    z = [_dot_nt(q_sb[h], jnp.concatenate([ks, mk_s], axis=0)) for h in heads]
    s = [_dot_nt(q_fx[h], jnp.concatenate([kf, mk_f], axis=0))
         + (ct[h] - jnp.concatenate([cs[h], mcs[h]], axis=1)) for h in heads]
    tk = ks.shape[0]
    z_d, z_m = [z[h][:, :tk] for h in heads], [z[h][:, tk:] for h in heads]
    parts_d = [_sb_split(z_d[h], strict) for h in heads]
    parts_m = [_sb_split(z_m[h], meta_valid) for h in heads]
    fx_mask = jnp.concatenate([causal, meta_valid], axis=1)
    s = [jnp.where(fx_mask, s[h], NEG_BIG) for h in heads]
    m_new = [jnp.max(s[h], axis=1, keepdims=True) for h in heads]
    p = [jnp.exp(s[h] - m_new[h]) for h in heads]
    cl_d = [_dot(parts_d[h][0], u) + _dot(parts_d[h][1], u) for h in heads]
    cl_m = [_dot(parts_m[h][0], u_meta) + _dot(parts_m[h][1], u_meta) for h in heads]
    v_all = jnp.concatenate([vf, mv_f], axis=0)
    pv_fx = [_dot(p[h].astype(BF16), v_all) for h in heads]
    zero = jnp.zeros_like(m_new[0])
    w_d = [_sb_weights(z_d[h], cl_d[h], zero, strict) for h in heads]
    e_m = [_sb_weights(z_m[h], cl_m[h], zero, meta_valid) for h in heads]
    for h in heads:
        meta_ref[h] = _dot(e_m[h], mv_s)
    new = [(_dot(w_d[h], vs), cl_d[h][:, 0:1]) for h in heads]
    new += [(pv_fx[h], m_new[h], jnp.sum(p[h], axis=1, keepdims=True)) for h in heads]
    return tuple(new)


def _sb_init(rows):
    return (jnp.zeros((rows, LANES), F32), jnp.zeros((rows, 1), F32))


def _fox_init(rows):
    return (jnp.zeros((rows, LANES), F32), jnp.full((rows, 1), NEG_BIG, F32),
            jnp.zeros((rows, 1), F32))


def _attn_kernel(qs_ref, ks_ref, vs_ref, qf_ref, kf_ref, vf_ref, ccol_ref, crow_ref,
                 mks_ref, mvs_ref, mkf_ref, mvf_ref, mcrow_ref, u_ref, osb_ref, ofx_ref, meta_ref):
    u = u_ref[...]
    row = lax.broadcasted_iota(jnp.int32, (TQ, TK), 0)
    col = lax.broadcasted_iota(jnp.int32, (TQ, TK), 1)
    strict = col < row
    causal = col <= row
    meta_valid = lax.broadcasted_iota(jnp.int32, (TQ, META_ROWS), 1) < N_META
    u_meta = u[:META_ROWS, :META_ROWS]

    def q_block(qi, _):
        q0 = pl.multiple_of(qi * TQ, TQ)
        q_sb = _head_queries(qs_ref[pl.ds(q0, TQ), :])
        q_fx = _head_queries(qf_ref[pl.ds(q0, TQ), :])
        ct = [ccol_ref[0, 0, pl.ds(q0, TQ), h:h + 1] for h in range(2)]

        def tile(k0, state, sb_mask, fx_mask):
            cs = [crow_ref[0, 0, h:h + 1, pl.ds(k0, TK)] for h in range(2)]
            return _attn_tile(q_sb, q_fx, ks_ref[pl.ds(k0, TK), :], vs_ref[pl.ds(k0, TK), :],
                              kf_ref[pl.ds(k0, TK), :], vf_ref[pl.ds(k0, TK), :], ct, cs, u,
                              state, sb_mask, fx_mask)

        state = _first_tile(
            q_sb, q_fx, ks_ref[pl.ds(q0, TK), :], vs_ref[pl.ds(q0, TK), :],
            kf_ref[pl.ds(q0, TK), :], vf_ref[pl.ds(q0, TK), :], mks_ref[...], mvs_ref[...],
            mkf_ref[...], mvf_ref[...], ct, [crow_ref[0, 0, h:h + 1, pl.ds(q0, TK)] for h in range(2)],
            [mcrow_ref[0, h:h + 1, :] for h in range(2)], u, u_meta, strict, causal, meta_valid,
            meta_ref)
        new = lax.fori_loop(
            0, qi, lambda i, s: tile(pl.multiple_of((qi - 1 - i) * TK, TK), s, None, None), state)
        sb_out = [new[h][0] + meta_ref[h] * jnp.exp(-new[h][1]) for h in range(2)]
        osb_ref[pl.ds(q0, TQ), :] = _merge_heads(sb_out[0], sb_out[1]).astype(BF16)
        ofx_ref[pl.ds(q0, TQ), :] = _merge_heads(new[2][0] / new[2][2],
                                                 new[3][0] / new[3][2]).astype(BF16)
        return 0

    lax.fori_loop(0, N_QBLK, q_block, 0)


def _meta_attn_kernel(qs_ref, ks_ref, vs_ref, qf_ref, kf_ref, vf_ref, ccol_ref, crow_ref, u_ref,
                      osb_ref, ofx_ref):
    row = lax.broadcasted_iota(jnp.int32, (META_ROWS, META_ROWS), 0)
    col = lax.broadcasted_iota(jnp.int32, (META_ROWS, META_ROWS), 1)
    state = (_sb_init(META_ROWS), _sb_init(META_ROWS), _fox_init(META_ROWS), _fox_init(META_ROWS))
    new = _attn_tile(_head_queries(qs_ref[...]), _head_queries(qf_ref[...]), ks_ref[...], vs_ref[...],
                     kf_ref[...], vf_ref[...], [ccol_ref[0, :, h:h + 1] for h in range(2)],
                     [crow_ref[0, h:h + 1, :] for h in range(2)], u_ref[...], state,
                     col < row, col <= row)
    osb_ref[...] = _merge_heads(new[0][0], new[1][0]).astype(BF16)
    ofx_ref[...] = _merge_heads(new[2][0] / new[2][2], new[3][0] / new[3][2]).astype(BF16)


def _outproj_kernel(h_ref, osb_ref, ofx_ref, gate_ref, wbs_ref, wbf_ref, wo_ref, g2_ref,
                    h1_ref, hn2_ref):
    t_sb = _dot(osb_ref[...], wbs_ref[...])
    t_fx = _dot(ofx_ref[...], wbf_ref[...])
    merged = (_sigmoid(gate_ref[:, :D_MODEL].astype(F32)) * t_sb
              + _sigmoid(gate_ref[:, D_MODEL:].astype(F32)) * t_fx)
    h1 = h_ref[...] + _dot(merged.astype(BF16), wo_ref[...])
    h1_ref[...] = h1
    hn2_ref[...] = ((h1 * _rms_scale(h1)) * g2_ref[...]).astype(BF16)


def _meta_up_kernel(hn2_ref, wup_ref, hist_ref):
    u = _dot(hn2_ref[0:N_META, :], wup_ref[...])
    hist_ref[...] = u[N_META - SUBLANES:, :]


def _ffn_kernel(hn2_ref, h1_ref, hist0_ref, wup_ref, cw_ref, wdn_ref, gf_ref, out_ref,
                u_ref, hist_ref, acc_ref):
    @pl.when(pl.program_id(1) == 0)
    def _():
        hist_ref[...] = hist0_ref[...]

    hn2 = hn2_ref[...]
    tm = hn2.shape[0]
    n_chunks = D_FF // FF_CHUNK
    u_ref[0:SUBLANES, :] = hist_ref[...]

    def chunk_cols(c):
        return [slice(half * D_FF + c * FF_CHUNK, half * D_FF + (c + 1) * FF_CHUNK)
                for half in range(2)]

    def up(c):
        for cols in chunk_cols(c):
            u_ref[SUBLANES:SUBLANES + tm, cols] = _dot(hn2, wup_ref[:, cols])

    def gate_down(c):
        a, b = [sum(cw_ref[j:j + 1, cols] * u_ref[SUBLANES - 2 + j:SUBLANES - 2 + j + tm, cols]
                    for j in range(CONV_W)) for cols in chunk_cols(c)]
        gated = (a * _sigmoid(a) * b).astype(BF16)
        contrib = _dot(gated, wdn_ref[c * FF_CHUNK:(c + 1) * FF_CHUNK, :])
        if c == 0:
            acc_ref[...] = contrib
        else:
            acc_ref[...] += contrib

    for c in range(n_chunks):
        up(c)
    for c in range(n_chunks):
        gate_down(c)
    hist_ref[...] = u_ref[tm:tm + SUBLANES, :]
    h2 = h1_ref[...] + acc_ref[...]
    out_ref[...] = (h2 * _rms_scale(h2)) * gf_ref[...]


def _const_spec(shape):
    return pl.BlockSpec(shape, lambda *_: (0,) * len(shape))


def _params(*semantics):
    return pltpu.CompilerParams(dimension_semantics=semantics, vmem_limit_bytes=VMEM_LIMIT)


def _inproj(h, tm, g, w_qkv, w_g, w_f, b_f, name):
    rows = h.shape[0]
    return pl.pallas_call(
        _inproj_kernel,
        grid=(rows // tm,),
        in_specs=[
            pl.BlockSpec((tm, D_MODEL), lambda i: (i, 0)),
            _const_spec((1, D_MODEL)),
            _const_spec((D_MODEL, QKV_COLS)),
            _const_spec((D_MODEL, 2 * D_MODEL)),
            _const_spec((D_MODEL, LANES)),
            _const_spec((1, LANES)),
        ],
        out_specs=[
            pl.BlockSpec((tm, QKV_COLS), lambda i: (i, 0)),
            pl.BlockSpec((tm, 2 * D_MODEL), lambda i: (i, 0)),
            pl.BlockSpec((tm, LANES), lambda i: (i, 0)),
        ],
        out_shape=[
            jax.ShapeDtypeStruct((rows, QKV_COLS), BF16),
            jax.ShapeDtypeStruct((rows, 2 * D_MODEL), BF16),
            jax.ShapeDtypeStruct((rows, LANES), F32),
        ],
        compiler_params=_params("parallel"),
        name=name,
    )(h, g, w_qkv, w_g, w_f, b_f)


def _cumsum(lf, seq_rows, init, tri, name):
    rows = lf.shape[0]
    return pl.pallas_call(
        _cumsum_kernel,
        grid=(rows // seq_rows,),
        in_specs=[pl.BlockSpec((seq_rows, LANES), lambda b: (b, 0)), _const_spec((1, LANES)),
                  _const_spec(tri.shape)],
        out_specs=pl.BlockSpec((seq_rows, LANES), lambda b: (b, 0)),
        out_shape=jax.ShapeDtypeStruct((rows, LANES), F32),
        compiler_params=_params("parallel"),
        name=name,
    )(lf, init, tri)


def _outproj(h, tm, o_sb, o_fx, gate, wb_sb, wb_fx, wo, g2, name):
    rows = h.shape[0]
    return pl.pallas_call(
        _outproj_kernel,
        grid=(rows // tm,),
        in_specs=[
            pl.BlockSpec((tm, D_MODEL), lambda i: (i, 0)),
            pl.BlockSpec((tm, W_BRANCH), lambda i: (i, 0)),
            pl.BlockSpec((tm, W_BRANCH), lambda i: (i, 0)),
            pl.BlockSpec((tm, 2 * D_MODEL), lambda i: (i, 0)),
            _const_spec((W_BRANCH, D_MODEL)),
            _const_spec((W_BRANCH, D_MODEL)),
            _const_spec((D_MODEL, D_MODEL)),
            _const_spec((1, D_MODEL)),
        ],
        out_specs=[
            pl.BlockSpec((tm, D_MODEL), lambda i: (i, 0)),
            pl.BlockSpec((tm, D_MODEL), lambda i: (i, 0)),
        ],
        out_shape=[
            jax.ShapeDtypeStruct((rows, D_MODEL), F32),
            jax.ShapeDtypeStruct((rows, D_MODEL), BF16),
        ],
        compiler_params=_params("parallel"),
        name=name,
    )(h, o_sb, o_fx, gate, wb_sb, wb_fx, wo, g2)


def _pair_views(cum, batch, seq_rows):
    c = cum[:, :N_HEADS].reshape(batch, seq_rows, HEAD_PAIRS, 2)
    return c.transpose(0, 2, 1, 3), c.transpose(0, 2, 3, 1)


def kernel(x, meta_tokens, norm_mix_g, w_in, b_forget, w_branch_sb, w_branch_fox, w_out,
           norm_ffn_g, w_up, conv_w, w_down, norm_final_g):
    batch = x.shape[0]
    assert x.shape == (batch, SEQ, D_MODEL) and w_in.shape[0] == 1
    rows = batch * SEQ

    w_in0 = w_in[0]
    w_qkv = w_in0[:, :QKV_COLS].astype(BF16)
    w_f = jnp.pad(w_in0[:, QKV_COLS:QKV_COLS + N_HEADS], ((0, 0), (0, LANES - N_HEADS))).astype(BF16)
    w_g = w_in0[:, QKV_COLS + N_HEADS:].astype(BF16)
    b_f = jnp.pad(b_forget[0].astype(F32), (0, LANES - N_HEADS)).reshape(1, LANES)
    g1 = norm_mix_g[0].reshape(1, D_MODEL)
    g2 = norm_ffn_g[0].reshape(1, D_MODEL)
    wb_sb, wb_fx, wo = (w_branch_sb[0].astype(BF16), w_branch_fox[0].astype(BF16),
                        w_out[0].astype(BF16))
    w_up_b, w_down_b = w_up[0].astype(BF16), w_down[0].astype(BF16)

    idx = jnp.arange(TK)
    u_mat = (idx[:, None] >= idx[None, :]).astype(BF16)
    tri = u_mat[:META_ROWS, :META_ROWS]
    h_real = x.reshape(rows, D_MODEL)
    h_meta = jnp.pad(meta_tokens.astype(x.dtype), ((0, META_ROWS - N_META), (0, 0)))

    qkv_m, gate_m, lf_m = _inproj(h_meta, META_ROWS, g1, w_qkv, w_g, w_f, b_f, "inproj_meta")
    cum_m = _cumsum(lf_m, META_ROWS, jnp.zeros((1, LANES), F32), tri, "forget_cumsum_meta")
    mc_col, mc_row = _pair_views(cum_m, 1, META_ROWS)
    mc_col, mc_row = mc_col[0], mc_row[0]

    def meta_blk(base):
        return pl.BlockSpec((META_ROWS, LANES), lambda p: (0, base + p))

    o_sb_m, o_fx_m = pl.pallas_call(
        _meta_attn_kernel,
        grid=(HEAD_PAIRS,),
        in_specs=[meta_blk(c * HEAD_PAIRS) for c in range(6)] + [
            pl.BlockSpec((1, META_ROWS, 2), lambda p: (p, 0, 0)),
            pl.BlockSpec((1, 2, META_ROWS), lambda p: (p, 0, 0)),
            _const_spec((META_ROWS, META_ROWS))],
        out_specs=[pl.BlockSpec((META_ROWS, LANES), lambda p: (0, p))] * 2,
        out_shape=[jax.ShapeDtypeStruct((META_ROWS, W_BRANCH), BF16)] * 2,
        compiler_params=_params("parallel"),
        name="attn_meta",
    )(qkv_m, qkv_m, qkv_m, qkv_m, qkv_m, qkv_m, mc_col, mc_row, u_mat[:META_ROWS, :META_ROWS])

    qkv, gate, lf = _inproj(h_real, TM, g1, w_qkv, w_g, w_f, b_f, "inproj")
    cum = _cumsum(lf, SEQ, cum_m[N_META - 1:N_META, :], tri, "forget_cumsum")
    c_col, c_row = _pair_views(cum, batch, SEQ)

    def seq_blk(base):
        return pl.BlockSpec((SEQ, LANES), lambda b, p: (b, base + p))

    def meta_kv(base):
        return pl.BlockSpec((META_ROWS, LANES), lambda b, p: (0, base + p))

    o_sb, o_fx = pl.pallas_call(
        _attn_kernel,
        grid=(batch, HEAD_PAIRS),
        in_specs=[seq_blk(c * HEAD_PAIRS) for c in range(6)] + [
            pl.BlockSpec((1, 1, SEQ, 2), lambda b, p: (b, p, 0, 0)),
            pl.BlockSpec((1, 1, 2, SEQ), lambda b, p: (b, p, 0, 0)),
            meta_kv(1 * HEAD_PAIRS), meta_kv(2 * HEAD_PAIRS),
            meta_kv(4 * HEAD_PAIRS), meta_kv(5 * HEAD_PAIRS),
            pl.BlockSpec((1, 2, META_ROWS), lambda b, p: (p, 0, 0)),
            pl.BlockSpec((TK, TK), lambda b, p: (0, 0))],
        out_specs=[pl.BlockSpec((SEQ, LANES), lambda b, p: (b, p))] * 2,
        out_shape=[jax.ShapeDtypeStruct((rows, W_BRANCH), BF16)] * 2,
        scratch_shapes=[pltpu.VMEM((2, TQ, LANES), F32)],
        compiler_params=_params("parallel", "parallel"),
        name="attn",
    )(qkv, qkv, qkv, qkv, qkv, qkv, c_col, c_row, qkv_m, qkv_m, qkv_m, qkv_m, mc_row, u_mat)

    _, hn2_m = _outproj(h_meta, META_ROWS, o_sb_m, o_fx_m, gate_m, wb_sb, wb_fx, wo, g2,
                        "outproj_meta")
    h1, hn2 = _outproj(h_real, TM, o_sb, o_fx, gate, wb_sb, wb_fx, wo, g2, "outproj")

    up_cols = 512
    hist0 = pl.pallas_call(
        _meta_up_kernel,
        grid=(2 * D_FF // up_cols,),
        in_specs=[_const_spec((META_ROWS, D_MODEL)),
                  pl.BlockSpec((D_MODEL, up_cols), lambda c: (0, c))],
        out_specs=pl.BlockSpec((SUBLANES, up_cols), lambda c: (0, c)),
        out_shape=jax.ShapeDtypeStruct((SUBLANES, 2 * D_FF), F32),
        compiler_params=_params("parallel"),
        name="up_meta",
    )(hn2_m, w_up_b)

    tiles = SEQ // TM
    out = pl.pallas_call(
        _ffn_kernel,
        grid=(batch, tiles),
        in_specs=[
            pl.BlockSpec((TM, D_MODEL), lambda b, t: (b * tiles + t, 0)),
            pl.BlockSpec((TM, D_MODEL), lambda b, t: (b * tiles + t, 0)),
            _const_spec((SUBLANES, 2 * D_FF)),
            pl.BlockSpec((D_MODEL, 2 * D_FF), lambda b, t: (0, 0), pipeline_mode=pl.Buffered(1)),
            _const_spec((CONV_W, 2 * D_FF)),
            pl.BlockSpec((D_FF, D_MODEL), lambda b, t: (0, 0), pipeline_mode=pl.Buffered(1)),
            _const_spec((1, D_MODEL)),
        ],
        out_specs=pl.BlockSpec((TM, D_MODEL), lambda b, t: (b * tiles + t, 0)),
        out_shape=jax.ShapeDtypeStruct((rows, D_MODEL), F32),
        scratch_shapes=[
            pltpu.VMEM((SUBLANES + TM, 2 * D_FF), F32),
            pltpu.VMEM((SUBLANES, 2 * D_FF), F32),
            pltpu.VMEM((TM, D_MODEL), F32),
        ],
        compiler_params=_params("arbitrary", "arbitrary"),
        name="conv_ffn",
    )(hn2, h1, hist0, w_up_b, conv_w[0].astype(F32), w_down_b, norm_final_g.reshape(1, D_MODEL))

    return out.reshape(batch, SEQ, D_MODEL)
```

```python
import jax
import jax.numpy as jnp
from jax import lax
from jax.experimental import pallas as pl
from jax.experimental.pallas import tpu as pltpu

D_MODEL = 1024
SEQ = 2048
N_META = 16
HEAD_DIM = 64
N_HEADS = 8
W_BRANCH = N_HEADS * HEAD_DIM
D_FF = 2816
CONV_W = 3
RMS_EPS = 1e-6

LANES = 128
SUBLANES = 8
META_ROWS = 128
HEAD_PAIRS = N_HEADS // 2
QKV_COLS = 6 * W_BRANCH
TQ = 256
TK = 256
N_QBLK = SEQ // TQ
TM = 512
FF_CHUNK = 256
VMEM_LIMIT = 56 * 1024 * 1024

F32 = jnp.float32
BF16 = jnp.bfloat16
NEG_BIG = -1e30


def _dot(a, b):
    return jnp.dot(a, b, preferred_element_type=F32)


def _dot_nt(a, b):
    return lax.dot_general(a, b, (((1,), (1,)), ((), ())), preferred_element_type=F32)


def _rms_scale(x):
    return lax.rsqrt(jnp.mean(x * x, axis=-1, keepdims=True) + RMS_EPS)


def _sigmoid(x):
    return 1.0 / (1.0 + jnp.exp(-x))


def _split3(x):
    hi = x.astype(BF16)
    r1 = x - hi.astype(F32)
    mid = r1.astype(BF16)
    lo = (r1 - mid.astype(F32)).astype(BF16)
    return hi, mid, lo


def _inproj_kernel(h_ref, g_ref, wqkv_ref, wg_ref, wf_ref, bf_ref, qkv_ref, gate_ref, lf_ref):
    x = h_ref[...]
    hn = ((x * _rms_scale(x)) * g_ref[...]).astype(BF16)
    for c in range(6):
        cols = slice(c * W_BRANCH, (c + 1) * W_BRANCH)
        y = _dot(hn, wqkv_ref[:, cols])
        if c in (0, 3):
            y = y * (HEAD_DIM ** -0.5)
        qkv_ref[:, cols] = y.astype(BF16)
    gate_ref[...] = _dot(hn, wg_ref[...]).astype(BF16)
    f = _dot(hn, wf_ref[...]) + bf_ref[...]
    lf_ref[...] = jnp.minimum(f, 0.0) - jnp.log(1.0 + jnp.exp(-jnp.abs(f)))


def _cumsum_kernel(lf_ref, init_ref, tri_ref, c_ref):
    tri = tri_ref[...]
    blk = tri.shape[0]
    carry = init_ref[...]
    for j in range(lf_ref.shape[0] // blk):
        rows = slice(j * blk, (j + 1) * blk)
        hi, mid, lo = _split3(lf_ref[rows, :])
        cs = (_dot(tri, hi) + _dot(tri, mid)) + _dot(tri, lo) + carry
        c_ref[rows, :] = cs
        carry = cs[blk - 1:blk, :]


def _head_queries(q2):
    q2 = q2.astype(F32)
    lane = lax.broadcasted_iota(jnp.int32, q2.shape, 1)
    first = lane < HEAD_DIM
    return (jnp.where(first, q2, 0.0).astype(BF16), jnp.where(first, 0.0, q2).astype(BF16))


def _merge_heads(acc0, acc1):
    lane = lax.broadcasted_iota(jnp.int32, acc0.shape, 1)
    return jnp.where(lane < HEAD_DIM, acc0, acc1)


def _sb_split(z, mask):
    nlk = jnp.maximum(z, 0.0) + jnp.log(1.0 + jnp.exp(-jnp.abs(z)))
    if mask is not None:
        nlk = jnp.where(mask, nlk, 0.0)
    hi = nlk.astype(BF16)
    return hi, (nlk - hi.astype(F32)).astype(BF16)


def _sb_weights(z, cl, carry, mask):
    w = jnp.exp(z - cl - carry)
    if mask is not None:
        w = jnp.where(mask, w, 0.0)
    return w.astype(BF16)


def _attn_tile(q_sb, q_fx, ks, vs, kf, vf, ct, cs, u, state, sb_mask, fx_mask):
    heads = range(2)
    z = [_dot_nt(q_sb[h], ks) for h in heads]
    s = [_dot_nt(q_fx[h], kf) + (ct[h] - cs[h]) for h in heads]
    parts = [_sb_split(z[h], sb_mask) for h in heads]
    if fx_mask is not None:
        s = [jnp.where(fx_mask, s[h], NEG_BIG) for h in heads]
    m_new = [jnp.maximum(state[2 + h][1], jnp.max(s[h], axis=1, keepdims=True)) for h in heads]
    alpha = [jnp.exp(state[2 + h][1] - m_new[h]) for h in heads]
    p = [jnp.exp(s[h] - m_new[h]) for h in heads]
    blk = u.shape[0]
    blocks = [slice(b * blk, (b + 1) * blk) for b in reversed(range(ks.shape[0] // blk))]
    cl = [[_dot(parts[h][0][:, cols], u) + _dot(parts[h][1][:, cols], u) for cols in blocks]
          for h in heads]
    pv_fx = [_dot(p[h].astype(BF16), vf) for h in heads]
    w, carry = [], []
    for h in heads:
        c, w_blocks = state[h][1], []
        for cols, cl_b in zip(blocks, cl[h]):
            mask_b = None if sb_mask is None else sb_mask[:, cols]
            w_blocks.append(_sb_weights(z[h][:, cols], cl_b, c, mask_b))
            c = c + cl_b[:, 0:1]
        w.append(w_blocks[0] if len(w_blocks) == 1 else jnp.concatenate(w_blocks[::-1], axis=1))
        carry.append(c)
    pv_sb = [_dot(w[h], vs) for h in heads]
    new = [(state[h][0] + pv_sb[h], carry[h]) for h in heads]
    new += [(alpha[h] * state[2 + h][0] + pv_fx[h], m_new[h],
             alpha[h] * state[2 + h][2] + jnp.sum(p[h], axis=1, keepdims=True)) for h in heads]
    return tuple(new)


def _first_tile(q_sb, q_fx, ks, vs, kf, vf, mk_s, mv_s, mk_f, mv_f, ct, cs, mcs, u, u_meta,
                strict, causal, meta_valid, meta_ref):
    heads = range(2)
    z = [_dot_nt(q_sb[h], jnp.concatenate([ks, mk_s], axis=0)) for h in heads]
    s = [_dot_nt(q_fx[h], jnp.concatenate([kf, mk_f], axis=0))
         + (ct[h] - jnp.concatenate([cs[h], mcs[h]], axis=1)) for h in heads]
    tk = ks.shape[0]
    z_d, z_m = [z[h][:, :tk] for h in heads], [z[h][:, tk:] for h in heads]
    parts_d = [_sb_split(z_d[h], strict) for h in heads]
    parts_m = [_sb_split(z_m[h], meta_valid) for h in heads]
    fx_mask = jnp.concatenate([causal, meta_valid], axis=1)
    s = [jnp.where(fx_mask, s[h], NEG_BIG) for h in heads]
    m_new = [jnp.max(s[h], axis=1, keepdims=True) for h in heads]
    p = [jnp.exp(s[h] - m_new[h]) for h in heads]
    cl_d = [_dot(parts_d[h][0], u) + _dot(parts_d[h][1], u) for h in heads]
    cl_m = [_dot(parts_m[h][0], u_meta) + _dot(parts_m[h][1], u_meta) for h in heads]
    v_all = jnp.concatenate([vf, mv_f], axis=0)
    pv_fx = [_dot(p[h].astype(BF16), v_all) for h in heads]
    zero = jnp.zeros_like(m_new[0])
    w_d = [_sb_weights(z_d[h], cl_d[h], zero, strict) for h in heads]
    e_m = [_sb_weights(z_m[h], cl_m[h], zero, meta_valid) for h in heads]
    for h in heads:
        meta_ref[h] = _dot(e_m[h], mv_s)
    new = [(_dot(w_d[h], vs), cl_d[h][:, 0:1]) for h in heads]
    new += [(pv_fx[h], m_new[h], jnp.sum(p[h], axis=1, keepdims=True)) for h in heads]
    return tuple(new)


def _sb_init(rows):
    return (jnp.zeros((rows, LANES), F32), jnp.zeros((rows, 1), F32))


def _fox_init(rows):
    return (jnp.zeros((rows, LANES), F32), jnp.full((rows, 1), NEG_BIG, F32),
            jnp.zeros((rows, 1), F32))


def _attn_kernel(qs_ref, ks_ref, vs_ref, qf_ref, kf_ref, vf_ref, ccol_ref, crow_ref,
                 mks_ref, mvs_ref, mkf_ref, mvf_ref, mcrow_ref, u_ref, osb_ref, ofx_ref, meta_ref):
    u = u_ref[...]
    row = lax.broadcasted_iota(jnp.int32, (TQ, TK), 0)
    col = lax.broadcasted_iota(jnp.int32, (TQ, TK), 1)
    strict = col < row
    causal = col <= row
    meta_valid = lax.broadcasted_iota(jnp.int32, (TQ, META_ROWS), 1) < N_META
    u_meta = u[:META_ROWS, :META_ROWS]

    def q_block(qi, _):
        q0 = pl.multiple_of(qi * TQ, TQ)
        q_sb = _head_queries(qs_ref[pl.ds(q0, TQ), :])
        q_fx = _head_queries(qf_ref[pl.ds(q0, TQ), :])
        ct = [ccol_ref[0, 0, pl.ds(q0, TQ), h:h + 1] for h in range(2)]

        def tile(k0, width, state):
            cs = [crow_ref[0, 0, h:h + 1, pl.ds(k0, width)] for h in range(2)]
            return _attn_tile(q_sb, q_fx, ks_ref[pl.ds(k0, width), :], vs_ref[pl.ds(k0, width), :],
                              kf_ref[pl.ds(k0, width), :], vf_ref[pl.ds(k0, width), :], ct, cs, u,
                              state, None, None)

        state = _first_tile(
            q_sb, q_fx, ks_ref[pl.ds(q0, TK), :], vs_ref[pl.ds(q0, TK), :],
            kf_ref[pl.ds(q0, TK), :], vf_ref[pl.ds(q0, TK), :], mks_ref[...], mvs_ref[...],
            mkf_ref[...], mvf_ref[...], ct, [crow_ref[0, 0, h:h + 1, pl.ds(q0, TK)] for h in range(2)],
            [mcrow_ref[0, h:h + 1, :] for h in range(2)], u, u_meta, strict, causal, meta_valid,
            meta_ref)
        odd = qi % 2
        state = lax.cond(odd == 1, lambda st: tile(pl.multiple_of(q0 - TK, TK), TK, st),
                         lambda st: st, state)
        pairs_end = q0 - odd * TK
        new = lax.fori_loop(
            0, qi // 2,
            lambda i, st: tile(pl.multiple_of(pairs_end - (i + 1) * 2 * TK, 2 * TK), 2 * TK, st),
            state)
        sb_out = [new[h][0] + meta_ref[h] * jnp.exp(-new[h][1]) for h in range(2)]
        osb_ref[pl.ds(q0, TQ), :] = _merge_heads(sb_out[0], sb_out[1]).astype(BF16)
        ofx_ref[pl.ds(q0, TQ), :] = _merge_heads(new[2][0] / new[2][2],
                                                 new[3][0] / new[3][2]).astype(BF16)
        return 0

    lax.fori_loop(0, N_QBLK, q_block, 0)


def _meta_attn_kernel(qs_ref, ks_ref, vs_ref, qf_ref, kf_ref, vf_ref, ccol_ref, crow_ref, u_ref,
                      osb_ref, ofx_ref):
    row = lax.broadcasted_iota(jnp.int32, (META_ROWS, META_ROWS), 0)
    col = lax.broadcasted_iota(jnp.int32, (META_ROWS, META_ROWS), 1)
    state = (_sb_init(META_ROWS), _sb_init(META_ROWS), _fox_init(META_ROWS), _fox_init(META_ROWS))
    new = _attn_tile(_head_queries(qs_ref[...]), _head_queries(qf_ref[...]), ks_ref[...], vs_ref[...],
                     kf_ref[...], vf_ref[...], [ccol_ref[0, :, h:h + 1] for h in range(2)],
                     [crow_ref[0, h:h + 1, :] for h in range(2)], u_ref[...], state,
                     col < row, col <= row)
    osb_ref[...] = _merge_heads(new[0][0], new[1][0]).astype(BF16)
    ofx_ref[...] = _merge_heads(new[2][0] / new[2][2], new[3][0] / new[3][2]).astype(BF16)


def _outproj_kernel(h_ref, osb_ref, ofx_ref, gate_ref, wbs_ref, wbf_ref, wo_ref, g2_ref,
                    h1_ref, hn2_ref):
    t_sb = _dot(osb_ref[...], wbs_ref[...])
    t_fx = _dot(ofx_ref[...], wbf_ref[...])
    merged = (_sigmoid(gate_ref[:, :D_MODEL].astype(F32)) * t_sb
              + _sigmoid(gate_ref[:, D_MODEL:].astype(F32)) * t_fx)
    h1 = h_ref[...] + _dot(merged.astype(BF16), wo_ref[...])
    h1_ref[...] = h1
    hn2_ref[...] = ((h1 * _rms_scale(h1)) * g2_ref[...]).astype(BF16)


def _meta_up_kernel(hn2_ref, wup_ref, hist_ref):
    u = _dot(hn2_ref[0:N_META, :], wup_ref[...])
    hist_ref[...] = u[N_META - SUBLANES:, :]


def _ffn_kernel(hn2_ref, h1_ref, hist0_ref, wup_ref, cw_ref, wdn_ref, gf_ref, out_ref,
                u_ref, hist_ref, acc_ref):
    @pl.when(pl.program_id(1) == 0)
    def _():
        hist_ref[...] = hist0_ref[...]

    hn2 = hn2_ref[...]
    tm = hn2.shape[0]
    n_chunks = D_FF // FF_CHUNK
    u_ref[0:SUBLANES, :] = hist_ref[...]

    def chunk_cols(c):
        return [slice(half * D_FF + c * FF_CHUNK, half * D_FF + (c + 1) * FF_CHUNK)
                for half in range(2)]

    def up(c):
        for cols in chunk_cols(c):
            u_ref[SUBLANES:SUBLANES + tm, cols] = _dot(hn2, wup_ref[:, cols])

    def gate_down(c):
        a, b = [sum(cw_ref[j:j + 1, cols] * u_ref[SUBLANES - 2 + j:SUBLANES - 2 + j + tm, cols]
                    for j in range(CONV_W)) for cols in chunk_cols(c)]
        gated = (a * _sigmoid(a) * b).astype(BF16)
        contrib = _dot(gated, wdn_ref[c * FF_CHUNK:(c + 1) * FF_CHUNK, :])
        if c == 0:
            acc_ref[...] = contrib
        else:
            acc_ref[...] += contrib

    for c in range(n_chunks):
        up(c)
    for c in range(n_chunks):
        gate_down(c)
    hist_ref[...] = u_ref[tm:tm + SUBLANES, :]
    h2 = h1_ref[...] + acc_ref[...]
    out_ref[...] = (h2 * _rms_scale(h2)) * gf_ref[...]


def _const_spec(shape):
    return pl.BlockSpec(shape, lambda *_: (0,) * len(shape))


def _params(*semantics):
    return pltpu.CompilerParams(dimension_semantics=semantics, vmem_limit_bytes=VMEM_LIMIT)


def _inproj(h, tm, g, w_qkv, w_g, w_f, b_f, name):
    rows = h.shape[0]
    return pl.pallas_call(
        _inproj_kernel,
        grid=(rows // tm,),
        in_specs=[
            pl.BlockSpec((tm, D_MODEL), lambda i: (i, 0)),
            _const_spec((1, D_MODEL)),
            _const_spec((D_MODEL, QKV_COLS)),
            _const_spec((D_MODEL, 2 * D_MODEL)),
            _const_spec((D_MODEL, LANES)),
            _const_spec((1, LANES)),
        ],
        out_specs=[
            pl.BlockSpec((tm, QKV_COLS), lambda i: (i, 0)),
            pl.BlockSpec((tm, 2 * D_MODEL), lambda i: (i, 0)),
            pl.BlockSpec((tm, LANES), lambda i: (i, 0)),
        ],
        out_shape=[
            jax.ShapeDtypeStruct((rows, QKV_COLS), BF16),
            jax.ShapeDtypeStruct((rows, 2 * D_MODEL), BF16),
            jax.ShapeDtypeStruct((rows, LANES), F32),
        ],
        compiler_params=_params("parallel"),
        name=name,
    )(h, g, w_qkv, w_g, w_f, b_f)


def _cumsum(lf, seq_rows, init, tri, name):
    rows = lf.shape[0]
    return pl.pallas_call(
        _cumsum_kernel,
        grid=(rows // seq_rows,),
        in_specs=[pl.BlockSpec((seq_rows, LANES), lambda b: (b, 0)), _const_spec((1, LANES)),
                  _const_spec(tri.shape)],
        out_specs=pl.BlockSpec((seq_rows, LANES), lambda b: (b, 0)),
        out_shape=jax.ShapeDtypeStruct((rows, LANES), F32),
        compiler_params=_params("parallel"),
        name=name,
    )(lf, init, tri)


def _outproj(h, tm, o_sb, o_fx, gate, wb_sb, wb_fx, wo, g2, name):
    rows = h.shape[0]
    return pl.pallas_call(
        _outproj_kernel,
        grid=(rows // tm,),
        in_specs=[
            pl.BlockSpec((tm, D_MODEL), lambda i: (i, 0)),
            pl.BlockSpec((tm, W_BRANCH), lambda i: (i, 0)),
            pl.BlockSpec((tm, W_BRANCH), lambda i: (i, 0)),
            pl.BlockSpec((tm, 2 * D_MODEL), lambda i: (i, 0)),
            _const_spec((W_BRANCH, D_MODEL)),
            _const_spec((W_BRANCH, D_MODEL)),
            _const_spec((D_MODEL, D_MODEL)),
            _const_spec((1, D_MODEL)),
        ],
        out_specs=[
            pl.BlockSpec((tm, D_MODEL), lambda i: (i, 0)),
            pl.BlockSpec((tm, D_MODEL), lambda i: (i, 0)),
        ],
        out_shape=[
            jax.ShapeDtypeStruct((rows, D_MODEL), F32),
            jax.ShapeDtypeStruct((rows, D_MODEL), BF16),
        ],
        compiler_params=_params("parallel"),
        name=name,
    )(h, o_sb, o_fx, gate, wb_sb, wb_fx, wo, g2)


def _pair_views(cum, batch, seq_rows):
    c = cum[:, :N_HEADS].reshape(batch, seq_rows, HEAD_PAIRS, 2)
    return c.transpose(0, 2, 1, 3), c.transpose(0, 2, 3, 1)


def kernel(x, meta_tokens, norm_mix_g, w_in, b_forget, w_branch_sb, w_branch_fox, w_out,
           norm_ffn_g, w_up, conv_w, w_down, norm_final_g):
    batch = x.shape[0]
    assert x.shape == (batch, SEQ, D_MODEL) and w_in.shape[0] == 1
    rows = batch * SEQ

    w_in0 = w_in[0]
    w_qkv = w_in0[:, :QKV_COLS].astype(BF16)
    w_f = jnp.pad(w_in0[:, QKV_COLS:QKV_COLS + N_HEADS], ((0, 0), (0, LANES - N_HEADS))).astype(BF16)
    w_g = w_in0[:, QKV_COLS + N_HEADS:].astype(BF16)
    b_f = jnp.pad(b_forget[0].astype(F32), (0, LANES - N_HEADS)).reshape(1, LANES)
    g1 = norm_mix_g[0].reshape(1, D_MODEL)
    g2 = norm_ffn_g[0].reshape(1, D_MODEL)
    wb_sb, wb_fx, wo = (w_branch_sb[0].astype(BF16), w_branch_fox[0].astype(BF16),
                        w_out[0].astype(BF16))
    w_up_b, w_down_b = w_up[0].astype(BF16), w_down[0].astype(BF16)

    idx = jnp.arange(TK)
    u_mat = (idx[:, None] >= idx[None, :]).astype(BF16)
    tri = u_mat[:META_ROWS, :META_ROWS]
    h_real = x.reshape(rows, D_MODEL)
    h_meta = jnp.pad(meta_tokens.astype(x.dtype), ((0, META_ROWS - N_META), (0, 0)))

    qkv_m, gate_m, lf_m = _inproj(h_meta, META_ROWS, g1, w_qkv, w_g, w_f, b_f, "inproj_meta")
    cum_m = _cumsum(lf_m, META_ROWS, jnp.zeros((1, LANES), F32), tri, "forget_cumsum_meta")
    mc_col, mc_row = _pair_views(cum_m, 1, META_ROWS)
    mc_col, mc_row = mc_col[0], mc_row[0]

    def meta_blk(base):
        return pl.BlockSpec((META_ROWS, LANES), lambda p: (0, base + p))

    o_sb_m, o_fx_m = pl.pallas_call(
        _meta_attn_kernel,
        grid=(HEAD_PAIRS,),
        in_specs=[meta_blk(c * HEAD_PAIRS) for c in range(6)] + [
            pl.BlockSpec((1, META_ROWS, 2), lambda p: (p, 0, 0)),
            pl.BlockSpec((1, 2, META_ROWS), lambda p: (p, 0, 0)),
            _const_spec((META_ROWS, META_ROWS))],
        out_specs=[pl.BlockSpec((META_ROWS, LANES), lambda p: (0, p))] * 2,
        out_shape=[jax.ShapeDtypeStruct((META_ROWS, W_BRANCH), BF16)] * 2,
        compiler_params=_params("parallel"),
        name="attn_meta",
    )(qkv_m, qkv_m, qkv_m, qkv_m, qkv_m, qkv_m, mc_col, mc_row, u_mat[:META_ROWS, :META_ROWS])

    qkv, gate, lf = _inproj(h_real, TM, g1, w_qkv, w_g, w_f, b_f, "inproj")
    cum = _cumsum(lf, SEQ, cum_m[N_META - 1:N_META, :], tri, "forget_cumsum")
    c_col, c_row = _pair_views(cum, batch, SEQ)

    def seq_blk(base):
        return pl.BlockSpec((SEQ, LANES), lambda b, p: (b, base + p))

    def meta_kv(base):
        return pl.BlockSpec((META_ROWS, LANES), lambda b, p: (0, base + p))

    o_sb, o_fx = pl.pallas_call(
        _attn_kernel,
        grid=(batch, HEAD_PAIRS),
        in_specs=[seq_blk(c * HEAD_PAIRS) for c in range(6)] + [
            pl.BlockSpec((1, 1, SEQ, 2), lambda b, p: (b, p, 0, 0)),
            pl.BlockSpec((1, 1, 2, SEQ), lambda b, p: (b, p, 0, 0)),
            meta_kv(1 * HEAD_PAIRS), meta_kv(2 * HEAD_PAIRS),
            meta_kv(4 * HEAD_PAIRS), meta_kv(5 * HEAD_PAIRS),
            pl.BlockSpec((1, 2, META_ROWS), lambda b, p: (p, 0, 0)),
            pl.BlockSpec((TK, TK), lambda b, p: (0, 0))],
        out_specs=[pl.BlockSpec((SEQ, LANES), lambda b, p: (b, p))] * 2,
        out_shape=[jax.ShapeDtypeStruct((rows, W_BRANCH), BF16)] * 2,
        scratch_shapes=[pltpu.VMEM((2, TQ, LANES), F32)],
        compiler_params=_params("parallel", "parallel"),
        name="attn",
    )(qkv, qkv, qkv, qkv, qkv, qkv, c_col, c_row, qkv_m, qkv_m, qkv_m, qkv_m, mc_row, u_mat)

    _, hn2_m = _outproj(h_meta, META_ROWS, o_sb_m, o_fx_m, gate_m, wb_sb, wb_fx, wo, g2,
                        "outproj_meta")
    h1, hn2 = _outproj(h_real, TM, o_sb, o_fx, gate, wb_sb, wb_fx, wo, g2, "outproj")

    up_cols = 512
    hist0 = pl.pallas_call(
        _meta_up_kernel,
        grid=(2 * D_FF // up_cols,),
        in_specs=[_const_spec((META_ROWS, D_MODEL)),
                  pl.BlockSpec((D_MODEL, up_cols), lambda c: (0, c))],
        out_specs=pl.BlockSpec((SUBLANES, up_cols), lambda c: (0, c)),
        out_shape=jax.ShapeDtypeStruct((SUBLANES, 2 * D_FF), F32),
        compiler_params=_params("parallel"),
        name="up_meta",
    )(hn2_m, w_up_b)

    tiles = SEQ // TM
    out = pl.pallas_call(
        _ffn_kernel,
        grid=(batch, tiles),
        in_specs=[
            pl.BlockSpec((TM, D_MODEL), lambda b, t: (b * tiles + t, 0)),
            pl.BlockSpec((TM, D_MODEL), lambda b, t: (b * tiles + t, 0)),
            _const_spec((SUBLANES, 2 * D_FF)),
            pl.BlockSpec((D_MODEL, 2 * D_FF), lambda b, t: (0, 0), pipeline_mode=pl.Buffered(1)),
            _const_spec((CONV_W, 2 * D_FF)),
            pl.BlockSpec((D_FF, D_MODEL), lambda b, t: (0, 0), pipeline_mode=pl.Buffered(1)),
            _const_spec((1, D_MODEL)),
        ],
        out_specs=pl.BlockSpec((TM, D_MODEL), lambda b, t: (b * tiles + t, 0)),
        out_shape=jax.ShapeDtypeStruct((rows, D_MODEL), F32),
        scratch_shapes=[
            pltpu.VMEM((SUBLANES + TM, 2 * D_FF), F32),
            pltpu.VMEM((SUBLANES, 2 * D_FF), F32),
            pltpu.VMEM((TM, D_MODEL), F32),
        ],
        compiler_params=_params("arbitrary", "arbitrary"),
        name="conv_ffn",
    )(hn2, h1, hist0, w_up_b, conv_w[0].astype(F32), w_down_b, norm_final_g.reshape(1, D_MODEL))

    return out.reshape(batch, SEQ, D_MODEL)
```

```python
import jax
import jax.numpy as jnp
from jax import lax
from jax.experimental import pallas as pl
from jax.experimental.pallas import tpu as pltpu

D_MODEL = 1024
SEQ = 2048
N_META = 16
HEAD_DIM = 64
N_HEADS = 8
W_BRANCH = N_HEADS * HEAD_DIM
D_FF = 2816
CONV_W = 3
RMS_EPS = 1e-6

LANES = 128
SUBLANES = 8
META_ROWS = 128
HEAD_PAIRS = N_HEADS // 2
QKV_COLS = 6 * W_BRANCH
TQ = 256
TK = 256
N_QBLK = SEQ // TQ
TM = 512
FF_CHUNK = 256
VMEM_LIMIT = 56 * 1024 * 1024

F32 = jnp.float32
BF16 = jnp.bfloat16
NEG_BIG = -1e30


def _dot(a, b):
    return jnp.dot(a, b, preferred_element_type=F32)


def _dot_nt(a, b):
    return lax.dot_general(a, b, (((1,), (1,)), ((), ())), preferred_element_type=F32)


def _rms_scale(x):
    return lax.rsqrt(jnp.mean(x * x, axis=-1, keepdims=True) + RMS_EPS)


def _sigmoid(x):
    return 1.0 / (1.0 + jnp.exp(-x))


def _split3(x):
    hi = x.astype(BF16)
    r1 = x - hi.astype(F32)
    mid = r1.astype(BF16)
    lo = (r1 - mid.astype(F32)).astype(BF16)
    return hi, mid, lo


def _inproj_kernel(h_ref, g_ref, wqkv_ref, wg_ref, wf_ref, bf_ref, qkv_ref, gate_ref, lf_ref):
    x = h_ref[...]
    hn = ((x * _rms_scale(x)) * g_ref[...]).astype(BF16)
    for c in range(6):
        cols = slice(c * W_BRANCH, (c + 1) * W_BRANCH)
        y = _dot(hn, wqkv_ref[:, cols])
        if c in (0, 3):
            y = y * (HEAD_DIM ** -0.5)
        qkv_ref[:, cols] = y.astype(BF16)
    gate_ref[...] = _dot(hn, wg_ref[...]).astype(BF16)
    f = _dot(hn, wf_ref[...]) + bf_ref[...]
    lf_ref[...] = jnp.minimum(f, 0.0) - jnp.log(1.0 + jnp.exp(-jnp.abs(f)))


def _cumsum_kernel(lf_ref, init_ref, tri_ref, c_ref):
    tri = tri_ref[...]
    blk = tri.shape[0]
    carry = init_ref[...]
    for j in range(lf_ref.shape[0] // blk):
        rows = slice(j * blk, (j + 1) * blk)
        hi, mid, lo = _split3(lf_ref[rows, :])
        cs = (_dot(tri, hi) + _dot(tri, mid)) + _dot(tri, lo) + carry
        c_ref[rows, :] = cs
        carry = cs[blk - 1:blk, :]


def _head_queries(q2):
    q2 = q2.astype(F32)
    lane = lax.broadcasted_iota(jnp.int32, q2.shape, 1)
    first = lane < HEAD_DIM
    return (jnp.where(first, q2, 0.0).astype(BF16), jnp.where(first, 0.0, q2).astype(BF16))


def _merge_heads(x0, x1):
    lane = lax.broadcasted_iota(jnp.int32, (x0.shape[0], LANES), 1)
    return jnp.where(lane < HEAD_DIM, x0, x1)


def _sb_split(z, mask):
    nlk = jnp.maximum(z, 0.0) + jnp.log(1.0 + jnp.exp(-jnp.abs(z)))
    if mask is not None:
        nlk = jnp.where(mask, nlk, 0.0)
    hi = nlk.astype(BF16)
    return hi, (nlk - hi.astype(F32)).astype(BF16)


def _sb_weights(z, cl, carry, mask):
    w = jnp.exp(z - cl - carry)
    if mask is not None:
        w = jnp.where(mask, w, 0.0)
    return w.astype(BF16)


def _attn_tile(q_sb, q_fx, ks, vs, kf, vf, ct, cs, u, state, sb_mask, fx_mask):
    heads = range(2)
    z = [_dot_nt(q_sb[h], ks) for h in heads]
    s = [_dot_nt(q_fx[h], kf) + (ct[h] - cs[h]) for h in heads]
    parts = [_sb_split(z[h], sb_mask) for h in heads]
    if fx_mask is not None:
        s = [jnp.where(fx_mask, s[h], NEG_BIG) for h in heads]
    acc_sb, carry_in, acc_fx, m_old, l_old = state
    m_new = [jnp.maximum(m_old[h], jnp.max(s[h], axis=1, keepdims=True)) for h in heads]
    alpha = [jnp.exp(m_old[h] - m_new[h]) for h in heads]
    p = [jnp.exp(s[h] - m_new[h]) for h in heads]
    blk = u.shape[0]
    blocks = [slice(b * blk, (b + 1) * blk) for b in reversed(range(ks.shape[0] // blk))]
    cl = [[_dot(parts[h][0][:, cols], u) + _dot(parts[h][1][:, cols], u) for cols in blocks]
          for h in heads]
    pv_fx = [_dot(p[h].astype(BF16), vf) for h in heads]
    w, carry = [], []
    for h in heads:
        c, w_blocks = carry_in[h], []
        for cols, cl_b in zip(blocks, cl[h]):
            mask_b = None if sb_mask is None else sb_mask[:, cols]
            w_blocks.append(_sb_weights(z[h][:, cols], cl_b, c, mask_b))
            c = c + cl_b[:, 0:1]
        w.append(w_blocks[0] if len(w_blocks) == 1 else jnp.concatenate(w_blocks[::-1], axis=1))
        carry.append(c)
    pv_sb = [_dot(w[h], vs) for h in heads]
    l_new = [alpha[h] * l_old[h] + jnp.sum(p[h], axis=1, keepdims=True) for h in heads]
    return (acc_sb + _merge_heads(*pv_sb), tuple(carry),
            _merge_heads(*alpha) * acc_fx + _merge_heads(*pv_fx), tuple(m_new), tuple(l_new))


def _first_tile(q_sb, q_fx, ks, vs, kf, vf, mk_s, mv_s, mk_f, mv_f, ct, cs, mcs, u, u_meta,
                strict, causal, meta_valid, meta_ref):
    heads = range(2)
    z = [_dot_nt(q_sb[h], jnp.concatenate([ks, mk_s], axis=0)) for h in heads]
    s = [_dot_nt(q_fx[h], jnp.concatenate([kf, mk_f], axis=0))
         + (ct[h] - jnp.concatenate([cs[h], mcs[h]], axis=1)) for h in heads]
    tk = ks.shape[0]
    z_d, z_m = [z[h][:, :tk] for h in heads], [z[h][:, tk:] for h in heads]
    parts_d = [_sb_split(z_d[h], strict) for h in heads]
    parts_m = [_sb_split(z_m[h], meta_valid) for h in heads]
    fx_mask = jnp.concatenate([causal, meta_valid], axis=1)
    s = [jnp.where(fx_mask, s[h], NEG_BIG) for h in heads]
    m_new = [jnp.max(s[h], axis=1, keepdims=True) for h in heads]
    p = [jnp.exp(s[h] - m_new[h]) for h in heads]
    cl_d = [_dot(parts_d[h][0], u) + _dot(parts_d[h][1], u) for h in heads]
    cl_m = [_dot(parts_m[h][0], u_meta) + _dot(parts_m[h][1], u_meta) for h in heads]
    v_all = jnp.concatenate([vf, mv_f], axis=0)
    pv_fx = [_dot(p[h].astype(BF16), v_all) for h in heads]
    zero = jnp.zeros_like(m_new[0])
    w_d = [_sb_weights(z_d[h], cl_d[h], zero, strict) for h in heads]
    e_m = [_sb_weights(z_m[h], cl_m[h], zero, meta_valid) for h in heads]
    for h in heads:
        meta_ref[h] = _dot(e_m[h], mv_s)
    return (_merge_heads(*[_dot(w_d[h], vs) for h in heads]),
            tuple(cl_d[h][:, 0:1] for h in heads), _merge_heads(*pv_fx), tuple(m_new),
            tuple(jnp.sum(p[h], axis=1, keepdims=True) for h in heads))


def _attn_init(rows):
    zero_col = jnp.zeros((rows, 1), F32)
    zero_acc = jnp.zeros((rows, LANES), F32)
    return (zero_acc, (zero_col, zero_col), zero_acc,
            (jnp.full((rows, 1), NEG_BIG, F32),) * 2, (zero_col, zero_col))


def _attn_kernel(qs_ref, ks_ref, vs_ref, qf_ref, kf_ref, vf_ref, ccol_ref, crow_ref,
                 mks_ref, mvs_ref, mkf_ref, mvf_ref, mcrow_ref, u_ref, osb_ref, ofx_ref, meta_ref):
    u = u_ref[...]
    row = lax.broadcasted_iota(jnp.int32, (TQ, TK), 0)
    col = lax.broadcasted_iota(jnp.int32, (TQ, TK), 1)
    strict = col < row
    causal = col <= row
    meta_valid = lax.broadcasted_iota(jnp.int32, (TQ, META_ROWS), 1) < N_META
    u_meta = u[:META_ROWS, :META_ROWS]

    def q_block(qi, _):
        q0 = pl.multiple_of(qi * TQ, TQ)
        q_sb = _head_queries(qs_ref[pl.ds(q0, TQ), :])
        q_fx = _head_queries(qf_ref[pl.ds(q0, TQ), :])
        ct = [ccol_ref[0, 0, pl.ds(q0, TQ), h:h + 1] for h in range(2)]

        def tile(k0, width, state):
            cs = [crow_ref[0, 0, h:h + 1, pl.ds(k0, width)] for h in range(2)]
            return _attn_tile(q_sb, q_fx, ks_ref[pl.ds(k0, width), :], vs_ref[pl.ds(k0, width), :],
                              kf_ref[pl.ds(k0, width), :], vf_ref[pl.ds(k0, width), :], ct, cs, u,
                              state, None, None)

        state = _first_tile(
            q_sb, q_fx, ks_ref[pl.ds(q0, TK), :], vs_ref[pl.ds(q0, TK), :],
            kf_ref[pl.ds(q0, TK), :], vf_ref[pl.ds(q0, TK), :], mks_ref[...], mvs_ref[...],
            mkf_ref[...], mvf_ref[...], ct, [crow_ref[0, 0, h:h + 1, pl.ds(q0, TK)] for h in range(2)],
            [mcrow_ref[0, h:h + 1, :] for h in range(2)], u, u_meta, strict, causal, meta_valid,
            meta_ref)
        odd = qi % 2
        state = lax.cond(odd == 1, lambda st: tile(pl.multiple_of(q0 - TK, TK), TK, st),
                         lambda st: st, state)
        pairs_end = q0 - odd * TK
        new = lax.fori_loop(
            0, qi // 2,
            lambda i, st: tile(pl.multiple_of(pairs_end - (i + 1) * 2 * TK, 2 * TK), 2 * TK, st),
            state)
        acc_sb, carry, acc_fx, _, l = new
        meta = _merge_heads(*[meta_ref[h] * jnp.exp(-carry[h]) for h in range(2)])
        osb_ref[pl.ds(q0, TQ), :] = (acc_sb + meta).astype(BF16)
        ofx_ref[pl.ds(q0, TQ), :] = (acc_fx / _merge_heads(*l)).astype(BF16)
        return 0

    lax.fori_loop(0, N_QBLK, q_block, 0)


def _meta_attn_kernel(qs_ref, ks_ref, vs_ref, qf_ref, kf_ref, vf_ref, ccol_ref, crow_ref, u_ref,
                      osb_ref, ofx_ref):
    row = lax.broadcasted_iota(jnp.int32, (META_ROWS, META_ROWS), 0)
    col = lax.broadcasted_iota(jnp.int32, (META_ROWS, META_ROWS), 1)
    acc_sb, _, acc_fx, _, l = _attn_tile(_head_queries(qs_ref[...]), _head_queries(qf_ref[...]), ks_ref[...], vs_ref[...],
                     kf_ref[...], vf_ref[...], [ccol_ref[0, :, h:h + 1] for h in range(2)],
                     [crow_ref[0, h:h + 1, :] for h in range(2)], u_ref[...],
                     _attn_init(META_ROWS), col < row, col <= row)
    osb_ref[...] = acc_sb.astype(BF16)
    ofx_ref[...] = (acc_fx / _merge_heads(*l)).astype(BF16)


def _outproj_kernel(h_ref, osb_ref, ofx_ref, gate_ref, wbs_ref, wbf_ref, wo_ref, g2_ref,
                    h1_ref, hn2_ref):
    t_sb = _dot(osb_ref[...], wbs_ref[...])
    t_fx = _dot(ofx_ref[...], wbf_ref[...])
    merged = (_sigmoid(gate_ref[:, :D_MODEL].astype(F32)) * t_sb
              + _sigmoid(gate_ref[:, D_MODEL:].astype(F32)) * t_fx)
    h1 = h_ref[...] + _dot(merged.astype(BF16), wo_ref[...])
    h1_ref[...] = h1
    hn2_ref[...] = ((h1 * _rms_scale(h1)) * g2_ref[...]).astype(BF16)


def _meta_up_kernel(hn2_ref, wup_ref, hist_ref):
    u = _dot(hn2_ref[0:N_META, :], wup_ref[...])
    hist_ref[...] = u[N_META - SUBLANES:, :]


def _ffn_kernel(hn2_ref, h1_ref, hist0_ref, wup_ref, cw_ref, wdn_ref, gf_ref, out_ref,
                u_ref, hist_ref, acc_ref):
    @pl.when(pl.program_id(1) == 0)
    def _():
        hist_ref[...] = hist0_ref[...]

    hn2 = hn2_ref[...]
    tm = hn2.shape[0]
    n_chunks = D_FF // FF_CHUNK
    u_ref[0:SUBLANES, :] = hist_ref[...]

    def chunk_cols(c):
        return [slice(half * D_FF + c * FF_CHUNK, half * D_FF + (c + 1) * FF_CHUNK)
                for half in range(2)]

    def up(c):
        for cols in chunk_cols(c):
            u_ref[SUBLANES:SUBLANES + tm, cols] = _dot(hn2, wup_ref[:, cols])

    def gate_down(c):
        a, b = [sum(cw_ref[j:j + 1, cols] * u_ref[SUBLANES - 2 + j:SUBLANES - 2 + j + tm, cols]
                    for j in range(CONV_W)) for cols in chunk_cols(c)]
        gated = (a * _sigmoid(a) * b).astype(BF16)
        contrib = _dot(gated, wdn_ref[c * FF_CHUNK:(c + 1) * FF_CHUNK, :])
        if c == 0:
            acc_ref[...] = contrib
        else:
            acc_ref[...] += contrib

    for c in range(n_chunks):
        up(c)
    for c in range(n_chunks):
        gate_down(c)
    hist_ref[...] = u_ref[tm:tm + SUBLANES, :]
    h2 = h1_ref[...] + acc_ref[...]
    out_ref[...] = (h2 * _rms_scale(h2)) * gf_ref[...]


def _const_spec(shape):
    return pl.BlockSpec(shape, lambda *_: (0,) * len(shape))


def _params(*semantics):
    return pltpu.CompilerParams(dimension_semantics=semantics, vmem_limit_bytes=VMEM_LIMIT)


def _inproj(h, tm, g, w_qkv, w_g, w_f, b_f, name):
    rows = h.shape[0]
    return pl.pallas_call(
        _inproj_kernel,
        grid=(rows // tm,),
        in_specs=[
            pl.BlockSpec((tm, D_MODEL), lambda i: (i, 0)),
            _const_spec((1, D_MODEL)),
            _const_spec((D_MODEL, QKV_COLS)),
            _const_spec((D_MODEL, 2 * D_MODEL)),
            _const_spec((D_MODEL, LANES)),
            _const_spec((1, LANES)),
        ],
        out_specs=[
            pl.BlockSpec((tm, QKV_COLS), lambda i: (i, 0)),
            pl.BlockSpec((tm, 2 * D_MODEL), lambda i: (i, 0)),
            pl.BlockSpec((tm, LANES), lambda i: (i, 0)),
        ],
        out_shape=[
            jax.ShapeDtypeStruct((rows, QKV_COLS), BF16),
            jax.ShapeDtypeStruct((rows, 2 * D_MODEL), BF16),
            jax.ShapeDtypeStruct((rows, LANES), F32),
        ],
        compiler_params=_params("parallel"),
        name=name,
    )(h, g, w_qkv, w_g, w_f, b_f)


def _cumsum(lf, seq_rows, init, tri, name):
    rows = lf.shape[0]
    return pl.pallas_call(
        _cumsum_kernel,
        grid=(rows // seq_rows,),
        in_specs=[pl.BlockSpec((seq_rows, LANES), lambda b: (b, 0)), _const_spec((1, LANES)),
                  _const_spec(tri.shape)],
        out_specs=pl.BlockSpec((seq_rows, LANES), lambda b: (b, 0)),
        out_shape=jax.ShapeDtypeStruct((rows, LANES), F32),
        compiler_params=_params("parallel"),
        name=name,
    )(lf, init, tri)


def _outproj(h, tm, o_sb, o_fx, gate, wb_sb, wb_fx, wo, g2, name):
    rows = h.shape[0]
    return pl.pallas_call(
        _outproj_kernel,
        grid=(rows // tm,),
        in_specs=[
            pl.BlockSpec((tm, D_MODEL), lambda i: (i, 0)),
            pl.BlockSpec((tm, W_BRANCH), lambda i: (i, 0)),
            pl.BlockSpec((tm, W_BRANCH), lambda i: (i, 0)),
            pl.BlockSpec((tm, 2 * D_MODEL), lambda i: (i, 0)),
            _const_spec((W_BRANCH, D_MODEL)),
            _const_spec((W_BRANCH, D_MODEL)),
            _const_spec((D_MODEL, D_MODEL)),
            _const_spec((1, D_MODEL)),
        ],
        out_specs=[
            pl.BlockSpec((tm, D_MODEL), lambda i: (i, 0)),
            pl.BlockSpec((tm, D_MODEL), lambda i: (i, 0)),
        ],
        out_shape=[
            jax.ShapeDtypeStruct((rows, D_MODEL), F32),
            jax.ShapeDtypeStruct((rows, D_MODEL), BF16),
        ],
        compiler_params=_params("parallel"),
        name=name,
    )(h, o_sb, o_fx, gate, wb_sb, wb_fx, wo, g2)


def _pair_views(cum, batch, seq_rows):
    c = cum[:, :N_HEADS].reshape(batch, seq_rows, HEAD_PAIRS, 2)
    return c.transpose(0, 2, 1, 3), c.transpose(0, 2, 3, 1)


def kernel(x, meta_tokens, norm_mix_g, w_in, b_forget, w_branch_sb, w_branch_fox, w_out,
           norm_ffn_g, w_up, conv_w, w_down, norm_final_g):
    batch = x.shape[0]
    assert x.shape == (batch, SEQ, D_MODEL) and w_in.shape[0] == 1
    rows = batch * SEQ

    w_in0 = w_in[0]
    w_qkv = w_in0[:, :QKV_COLS].astype(BF16)
    w_f = jnp.pad(w_in0[:, QKV_COLS:QKV_COLS + N_HEADS], ((0, 0), (0, LANES - N_HEADS))).astype(BF16)
    w_g = w_in0[:, QKV_COLS + N_HEADS:].astype(BF16)
    b_f = jnp.pad(b_forget[0].astype(F32), (0, LANES - N_HEADS)).reshape(1, LANES)
    g1 = norm_mix_g[0].reshape(1, D_MODEL)
    g2 = norm_ffn_g[0].reshape(1, D_MODEL)
    wb_sb, wb_fx, wo = (w_branch_sb[0].astype(BF16), w_branch_fox[0].astype(BF16),
                        w_out[0].astype(BF16))
    w_up_b, w_down_b = w_up[0].astype(BF16), w_down[0].astype(BF16)

    idx = jnp.arange(TK)
    u_mat = (idx[:, None] >= idx[None, :]).astype(BF16)
    tri = u_mat[:META_ROWS, :META_ROWS]
    h_real = x.reshape(rows, D_MODEL)
    h_meta = jnp.pad(meta_tokens.astype(x.dtype), ((0, META_ROWS - N_META), (0, 0)))

    qkv_m, gate_m, lf_m = _inproj(h_meta, META_ROWS, g1, w_qkv, w_g, w_f, b_f, "inproj_meta")
    cum_m = _cumsum(lf_m, META_ROWS, jnp.zeros((1, LANES), F32), tri, "forget_cumsum_meta")
    mc_col, mc_row = _pair_views(cum_m, 1, META_ROWS)
    mc_col, mc_row = mc_col[0], mc_row[0]

    def meta_blk(base):
        return pl.BlockSpec((META_ROWS, LANES), lambda p: (0, base + p))

    o_sb_m, o_fx_m = pl.pallas_call(
        _meta_attn_kernel,
        grid=(HEAD_PAIRS,),
        in_specs=[meta_blk(c * HEAD_PAIRS) for c in range(6)] + [
            pl.BlockSpec((1, META_ROWS, 2), lambda p: (p, 0, 0)),
            pl.BlockSpec((1, 2, META_ROWS), lambda p: (p, 0, 0)),
            _const_spec((META_ROWS, META_ROWS))],
        out_specs=[pl.BlockSpec((META_ROWS, LANES), lambda p: (0, p))] * 2,
        out_shape=[jax.ShapeDtypeStruct((META_ROWS, W_BRANCH), BF16)] * 2,
        compiler_params=_params("parallel"),
        name="attn_meta",
    )(qkv_m, qkv_m, qkv_m, qkv_m, qkv_m, qkv_m, mc_col, mc_row, u_mat[:META_ROWS, :META_ROWS])

    qkv, gate, lf = _inproj(h_real, TM, g1, w_qkv, w_g, w_f, b_f, "inproj")
    cum = _cumsum(lf, SEQ, cum_m[N_META - 1:N_META, :], tri, "forget_cumsum")
    c_col, c_row = _pair_views(cum, batch, SEQ)

    def seq_blk(base):
        return pl.BlockSpec((SEQ, LANES), lambda b, p: (b, base + p))

    def meta_kv(base):
        return pl.BlockSpec((META_ROWS, LANES), lambda b, p: (0, base + p))

    o_sb, o_fx = pl.pallas_call(
        _attn_kernel,
        grid=(batch, HEAD_PAIRS),
        in_specs=[seq_blk(c * HEAD_PAIRS) for c in range(6)] + [
            pl.BlockSpec((1, 1, SEQ, 2), lambda b, p: (b, p, 0, 0)),
            pl.BlockSpec((1, 1, 2, SEQ), lambda b, p: (b, p, 0, 0)),
            meta_kv(1 * HEAD_PAIRS), meta_kv(2 * HEAD_PAIRS),
            meta_kv(4 * HEAD_PAIRS), meta_kv(5 * HEAD_PAIRS),
            pl.BlockSpec((1, 2, META_ROWS), lambda b, p: (p, 0, 0)),
            pl.BlockSpec((TK, TK), lambda b, p: (0, 0))],
        out_specs=[pl.BlockSpec((SEQ, LANES), lambda b, p: (b, p))] * 2,
        out_shape=[jax.ShapeDtypeStruct((rows, W_BRANCH), BF16)] * 2,
        scratch_shapes=[pltpu.VMEM((2, TQ, LANES), F32)],
        compiler_params=_params("parallel", "parallel"),
        name="attn",
    )(qkv, qkv, qkv, qkv, qkv, qkv, c_col, c_row, qkv_m, qkv_m, qkv_m, qkv_m, mc_row, u_mat)

    _, hn2_m = _outproj(h_meta, META_ROWS, o_sb_m, o_fx_m, gate_m, wb_sb, wb_fx, wo, g2,
                        "outproj_meta")
    h1, hn2 = _outproj(h_real, TM, o_sb, o_fx, gate, wb_sb, wb_fx, wo, g2, "outproj")

    up_cols = 512
    hist0 = pl.pallas_call(
        _meta_up_kernel,
        grid=(2 * D_FF // up_cols,),
        in_specs=[_const_spec((META_ROWS, D_MODEL)),
                  pl.BlockSpec((D_MODEL, up_cols), lambda c: (0, c))],
        out_specs=pl.BlockSpec((SUBLANES, up_cols), lambda c: (0, c)),
        out_shape=jax.ShapeDtypeStruct((SUBLANES, 2 * D_FF), F32),
        compiler_params=_params("parallel"),
        name="up_meta",
    )(hn2_m, w_up_b)

    tiles = SEQ // TM
    out = pl.pallas_call(
        _ffn_kernel,
        grid=(batch, tiles),
        in_specs=[
            pl.BlockSpec((TM, D_MODEL), lambda b, t: (b * tiles + t, 0)),
            pl.BlockSpec((TM, D_MODEL), lambda b, t: (b * tiles + t, 0)),
            _const_spec((SUBLANES, 2 * D_FF)),
            pl.BlockSpec((D_MODEL, 2 * D_FF), lambda b, t: (0, 0), pipeline_mode=pl.Buffered(1)),
            _const_spec((CONV_W, 2 * D_FF)),
            pl.BlockSpec((D_FF, D_MODEL), lambda b, t: (0, 0), pipeline_mode=pl.Buffered(1)),
            _const_spec((1, D_MODEL)),
        ],
        out_specs=pl.BlockSpec((TM, D_MODEL), lambda b, t: (b * tiles + t, 0)),
        out_shape=jax.ShapeDtypeStruct((rows, D_MODEL), F32),
        scratch_shapes=[
            pltpu.VMEM((SUBLANES + TM, 2 * D_FF), F32),
            pltpu.VMEM((SUBLANES, 2 * D_FF), F32),
            pltpu.VMEM((TM, D_MODEL), F32),
        ],
        compiler_params=_params("arbitrary", "arbitrary"),
        name="conv_ffn",
    )(hn2, h1, hist0, w_up_b, conv_w[0].astype(F32), w_down_b, norm_final_g.reshape(1, D_MODEL))

    return out.reshape(batch, SEQ, D_MODEL)
```
